```python
import math
import jax, jax.numpy as jnp
from jax import lax
import numpy as np

D_MODEL = 2048
BATCH = 16
SEQ = 2048
DEPTH = 2
DEC_BATCH = 2
DEC_SEQ = 8192
PAST_LEN = 128

GRID_W = 64
N_MEM = 256
X_HEADS = 4
X_HEAD_DIM = D_MODEL // X_HEADS
GLA_HEADS = 4
GLA_DK = D_MODEL // 2
GLA_DV = D_MODEL
GLA_HEAD_K = GLA_DK // GLA_HEADS
GLA_HEAD_V = GLA_DV // GLA_HEADS
GATE_RANK = 16
GATE_NORM = 16.0
CHUNK = 64
GLA_IN = 2 * GLA_DK + 2 * GLA_DV + 2 * GATE_RANK
HEAD_DIM = 128
N_Q_HEADS = D_MODEL // HEAD_DIM
N_KV_HEADS = N_Q_HEADS // 4
GQA_GROUP = N_Q_HEADS // N_KV_HEADS
KV_DIM = N_KV_HEADS * HEAD_DIM
AXIS_DIM = HEAD_DIM // 2
ROPE_THETA = 10000.0
Q_BLOCK = 128
D_FF = 4 * D_MODEL
N_GLA = (DEPTH + 1) // 2
N_ATT = DEPTH // 2
DN_ALPHA = (2.0 * DEPTH) ** 0.25
DN_BETA = (8.0 * DEPTH) ** -0.25
EPS = 1e-5

kernel_name = "hybrid_gla_gqa_encoder"


def layer_norm(x, g, b):
    xf = x.astype(jnp.float32)
    mu = jnp.mean(xf, -1, keepdims=True)
    var = jnp.mean(jnp.square(xf - mu), -1, keepdims=True)
    return ((xf - mu) * lax.rsqrt(var + EPS) * g.astype(jnp.float32) + b.astype(jnp.float32)).astype(x.dtype)


def rms_norm(x, g):
    xf = x.astype(jnp.float32)
    return (xf * lax.rsqrt(jnp.mean(xf * xf, -1, keepdims=True) + EPS) * g.astype(jnp.float32)).astype(x.dtype)


def gla_scan(q, k, v, g, strict):
    B, H, N, dk = q.shape
    dv = v.shape[-1]
    nc = N // CHUNK
    rs = lambda t: t.reshape(B, H, nc, CHUNK, t.shape[-1])
    q, k, v, g = rs(q), rs(k), rs(v), rs(g)
    bcum = jnp.cumsum(g, axis=3)
    blast = bcum[:, :, :, -1:, :]
    qe = q * jnp.exp(bcum)
    ke = k * jnp.exp(-bcum)
    kd = k * jnp.exp(blast - bcum)
    mask = jnp.tril(jnp.ones((CHUNK, CHUNK), dtype=bool), -1 if strict else 0)
    a = jnp.where(mask, jnp.einsum('bhcid,bhcjd->bhcij', qe, ke), 0.0)
    o_intra = jnp.einsum('bhcij,bhcjv->bhciv', a, v)

    def step(S, inp):
        qe_c, kd_c, v_c, dec_c = inp
        o = jnp.einsum('bhid,bhdv->bhiv', qe_c, S)
        S = S * dec_c[..., None] + jnp.einsum('bhjd,bhjv->bhdv', kd_c, v_c)
        return S, o

    mv = lambda t: jnp.moveaxis(t, 2, 0)
    S0 = jnp.zeros((B, H, dk, dv), jnp.float32)
    _, o_inter = lax.scan(step, S0, (mv(qe), mv(kd), mv(v), mv(jnp.exp(blast[:, :, :, 0, :]))))
    o = o_intra + jnp.moveaxis(o_inter, 0, 2)
    return o.reshape(B, H, N, dv)


def gla_mixer(x, w_in, w_gate_up, b_gate, norm_g, w_out):
    B, N, _ = x.shape
    proj = x @ w_in
    q, k, v, r, zf, zb = jnp.split(
        proj, [GLA_DK, 2 * GLA_DK, 2 * GLA_DK + GLA_DV, 2 * GLA_DK + 2 * GLA_DV,
               2 * GLA_DK + 2 * GLA_DV + GATE_RANK], axis=-1)

    def heads(t, hd):
        return t.reshape(B, N, GLA_HEADS, hd).transpose(0, 2, 1, 3).astype(jnp.float32)

    q = heads(q, GLA_HEAD_K) * (GLA_HEAD_K ** -0.5)
    k = heads(k, GLA_HEAD_K)
    v = heads(v, GLA_HEAD_V)
    gf = jax.nn.log_sigmoid((zf @ w_gate_up[0] + b_gate[0]).astype(jnp.float32)) / GATE_NORM
    gb = jax.nn.log_sigmoid((zb @ w_gate_up[1] + b_gate[1]).astype(jnp.float32)) / GATE_NORM
    gf = heads(gf, GLA_HEAD_K)
    gb = heads(gb, GLA_HEAD_K)
    o_f = gla_scan(q, k, v, gf, strict=False)
    flip = lambda t: jnp.flip(t, axis=2)
    o_b = flip(gla_scan(flip(q), flip(k), flip(v), flip(gb), strict=True))
    o = (o_f + o_b).astype(x.dtype)
    o = rms_norm(o, norm_g)
    o = o.transpose(0, 2, 1, 3).reshape(B, N, GLA_DV)
    o = o * jax.nn.silu(r)
    return o @ w_out


def axial_rope(n):
    rows = n // GRID_W
    row = jnp.repeat(jnp.arange(rows, dtype=jnp.float32), GRID_W)
    col = jnp.tile(jnp.arange(GRID_W, dtype=jnp.float32), rows)
    inv = ROPE_THETA ** (-jnp.arange(0, AXIS_DIM, 2, dtype=jnp.float32) / AXIS_DIM)
    ar = row[:, None] * inv
    ac = col[:, None] * inv
    ang = jnp.concatenate([ar, ar, ac, ac], axis=-1)
    return jnp.cos(ang), jnp.sin(ang)


def rotate_half(t):
    t1, t2 = jnp.split(t, 2, axis=-1)
    return jnp.concatenate([-t2, t1], axis=-1)


def apply_axial_rope(x, cos, sin):
    xr, xc = jnp.split(x, 2, axis=-1)
    rot = jnp.concatenate([rotate_half(xr), rotate_half(xc)], axis=-1)
    c, s = cos[:, None, :], sin[:, None, :]
    return (x.astype(jnp.float32) * c + rot.astype(jnp.float32) * s).astype(x.dtype)


def gqa_mixer(x, w_qkv, q_gain, k_gain, w_out):
    B, N, _ = x.shape
    qkv = x @ w_qkv
    q, k, v = jnp.split(qkv, [D_MODEL, D_MODEL + KV_DIM], axis=-1)
    q = rms_norm(q.reshape(B, N, N_Q_HEADS, HEAD_DIM), q_gain)
    k = rms_norm(k.reshape(B, N, N_KV_HEADS, HEAD_DIM), k_gain)
    v = v.reshape(B, N, N_KV_HEADS, HEAD_DIM)
    cos, sin = axial_rope(N)
    q = apply_axial_rope(q, cos, sin)
    k = apply_axial_rope(k, cos, sin)
    nb = N // Q_BLOCK
    qb = q.reshape(B, nb, Q_BLOCK, N_KV_HEADS, GQA_GROUP, HEAD_DIM).swapaxes(0, 1)
    scale = HEAD_DIM ** -0.5

    def attend(qblk):
        s = jnp.einsum('bqkgd,bskd->bkgqs', qblk, k).astype(jnp.float32) * scale
        p = jax.nn.softmax(s, axis=-1)
        return jnp.einsum('bkgqs,bskd->bqkgd', p.astype(v.dtype), v)

    o = lax.map(attend, qb)
    o = o.swapaxes(0, 1).reshape(B, N, D_MODEL)
    return o @ w_out


def mem_cross_attn(x, mem, w_q, w_kv, w_o):
    B, N, _ = x.shape
    M = mem.shape[1]
    q = (x @ w_q).reshape(B, N, X_HEADS, X_HEAD_DIM)
    k, v = jnp.split(mem @ w_kv, 2, axis=-1)
    k = k.reshape(B, M, X_HEADS, X_HEAD_DIM)
    v = v.reshape(B, M, X_HEADS, X_HEAD_DIM)
    s = jnp.einsum('bnhd,bmhd->bhnm', q, k).astype(jnp.float32) * (X_HEAD_DIM ** -0.5)
    p = jax.nn.softmax(s, axis=-1)
    o = jnp.einsum('bhnm,bmhd->bnhd', p.astype(v.dtype), v).reshape(B, N, D_MODEL)
    return o @ w_o


def sq_relu_mlp(x, w1, w2):
    return jnp.square(jax.nn.relu(x @ w1)) @ w2


def trunk(x, mem, gla_w_in, gla_w_gate_up, gla_b_gate, gla_norm_g, gla_w_out,
          att_w_qkv, att_q_gain, att_k_gain, att_w_out,
          mem_w_q, mem_w_kv, mem_w_o, mlp_w1, mlp_w2, ln_g, ln_b):
    for i in range(DEPTH):
        j = i // 2
        if i % 2 == 0:
            h = gla_mixer(x, gla_w_in[j], gla_w_gate_up[j], gla_b_gate[j], gla_norm_g[j], gla_w_out[j])
        else:
            h = gqa_mixer(x, att_w_qkv[j], att_q_gain[j], att_k_gain[j], att_w_out[j])
        x = layer_norm(DN_ALPHA * x + h, ln_g[i, 0], ln_b[i, 0])
        x = layer_norm(DN_ALPHA * x + mem_cross_attn(x, mem, mem_w_q[i], mem_w_kv[i], mem_w_o[i]),
                       ln_g[i, 1], ln_b[i, 1])
        x = layer_norm(DN_ALPHA * x + sq_relu_mlp(x, mlp_w1[i], mlp_w2[i]), ln_g[i, 2], ln_b[i, 2])
    return x


def setup_inputs(seed: int = 0) -> dict:
    key = jax.random.key(seed)
    ks = jax.random.split(key, 24)
    f32 = jnp.float32
    nrm = lambda k, shape, scale: jax.random.normal(k, shape, f32) * scale
    x_prompt = nrm(ks[0], (BATCH, SEQ, D_MODEL), 1.0)
    x_sample = nrm(ks[1], (DEC_BATCH, DEC_SEQ, D_MODEL), 1.0)
    mem_prompt = nrm(ks[2], (BATCH, N_MEM, D_MODEL), 1.0)
    mem_sample = nrm(ks[3], (DEC_BATCH, N_MEM, D_MODEL), 1.0)
    fan = D_MODEL ** -0.5
    col_scale = jnp.concatenate([
        jnp.ones((2 * GLA_DK,), f32), jnp.full((GLA_DV,), DN_BETA, f32),
        jnp.ones((GLA_DV + 2 * GATE_RANK,), f32)])
    gla_w_in = nrm(ks[4], (N_GLA, D_MODEL, GLA_IN), fan) * col_scale
    gla_w_gate_up = nrm(ks[5], (N_GLA, 2, GATE_RANK, GLA_DK), GATE_RANK ** -0.5)
    gla_b_gate = nrm(ks[6], (N_GLA, 2, GLA_DK), 0.1)
    gla_norm_g = 1.0 + nrm(ks[7], (N_GLA, GLA_HEAD_V), 0.01)
    gla_w_out = nrm(ks[8], (N_GLA, GLA_DV, D_MODEL), GLA_DV ** -0.5 * DN_BETA)
    qkv_scale = jnp.concatenate([jnp.ones((D_MODEL + KV_DIM,), f32), jnp.full((KV_DIM,), DN_BETA, f32)])
    att_w_qkv = nrm(ks[9], (N_ATT, D_MODEL, D_MODEL + 2 * KV_DIM), fan) * qkv_scale
    att_q_gain = 1.0 + nrm(ks[10], (N_ATT, HEAD_DIM), 0.01)
    att_k_gain = 1.0 + nrm(ks[11], (N_ATT, HEAD_DIM), 0.01)
    att_w_out = nrm(ks[12], (N_ATT, D_MODEL, D_MODEL), fan * DN_BETA)
    mem_w_q = nrm(ks[13], (DEPTH, D_MODEL, D_MODEL), fan)
    kv_scale = jnp.concatenate([jnp.ones((D_MODEL,), f32), jnp.full((D_MODEL,), DN_BETA, f32)])
    mem_w_kv = nrm(ks[14], (DEPTH, D_MODEL, 2 * D_MODEL), fan) * kv_scale
    mem_w_o = nrm(ks[15], (DEPTH, D_MODEL, D_MODEL), fan * DN_BETA)
    mlp_w1 = nrm(ks[16], (DEPTH, D_MODEL, D_FF), fan)
    mlp_w2 = nrm(ks[17], (DEPTH, D_FF, D_MODEL), D_FF ** -0.5 * DN_BETA)
    ln_g = 1.0 + nrm(ks[18], (DEPTH, 3, D_MODEL), 0.01)
    ln_b = nrm(ks[19], (DEPTH, 3, D_MODEL), 0.01)
    return {"x_prompt": x_prompt, "x_sample": x_sample, "mem_prompt": mem_prompt, "mem_sample": mem_sample,
            "gla_w_in": gla_w_in, "gla_w_gate_up": gla_w_gate_up, "gla_b_gate": gla_b_gate,
            "gla_norm_g": gla_norm_g, "gla_w_out": gla_w_out,
            "att_w_qkv": att_w_qkv, "att_q_gain": att_q_gain, "att_k_gain": att_k_gain, "att_w_out": att_w_out,
            "mem_w_q": mem_w_q, "mem_w_kv": mem_w_kv, "mem_w_o": mem_w_o,
            "mlp_w1": mlp_w1, "mlp_w2": mlp_w2, "ln_g": ln_g, "ln_b": ln_b}


def reference(x_prompt, x_sample, mem_prompt, mem_sample,
              gla_w_in, gla_w_gate_up, gla_b_gate, gla_norm_g, gla_w_out,
              att_w_qkv, att_q_gain, att_k_gain, att_w_out,
              mem_w_q, mem_w_kv, mem_w_o, mlp_w1, mlp_w2, ln_g, ln_b):
    y_prompt = trunk(x_prompt, mem_prompt, gla_w_in, gla_w_gate_up, gla_b_gate, gla_norm_g, gla_w_out,
                     att_w_qkv, att_q_gain, att_k_gain, att_w_out,
                     mem_w_q, mem_w_kv, mem_w_o, mlp_w1, mlp_w2, ln_g, ln_b)
    y_sample = trunk(x_sample, mem_sample, gla_w_in, gla_w_gate_up, gla_b_gate, gla_norm_g, gla_w_out,
                     att_w_qkv, att_q_gain, att_k_gain, att_w_out,
                     mem_w_q, mem_w_kv, mem_w_o, mlp_w1, mlp_w2, ln_g, ln_b)
    return (y_prompt, y_sample)
```

```python
import functools
import math

import jax
import jax.numpy as jnp
from jax import lax
from jax.experimental import pallas as pl
from jax.experimental.pallas import tpu as pltpu

F32 = jnp.float32
BF16 = jnp.bfloat16

DEPTH = 2
GRID_W = 64
X_HEADS = 4
GLA_HEADS = 4
GATE_RANK = 16
GATE_NORM = 16.0
CHUNK = 64
HEAD_DIM = 128
GQA_GROUP = 4
ROPE_THETA = 10000.0
DN_ALPHA = (2.0 * DEPTH) ** 0.25
EPS = 1e-5

LANES = 128
VMEM_LIMIT = 52 * 1024 * 1024


def _params(*sem):
    return pltpu.CompilerParams(dimension_semantics=sem, vmem_limit_bytes=VMEM_LIMIT)


def _dot(a, b):
    return jnp.dot(a, b, preferred_element_type=F32)


def _dot_nt(a, b):
    return lax.dot_general(a, b, (((1,), (1,)), ((), ())), preferred_element_type=F32)


def _dot_tn(a, b):
    return lax.dot_general(a, b, (((0,), (0,)), ((), ())), preferred_element_type=F32)


def _layer_norm(y, g, b):
    mu = jnp.mean(y, axis=-1, keepdims=True)
    d = y - mu
    var = jnp.mean(d * d, axis=-1, keepdims=True)
    return d * lax.rsqrt(var + EPS) * g + b


def _tile(n, want):
    t = min(n, want)
    assert n % t == 0, (n, t)
    return t


def _matmul_kernel(x_ref, w_ref, o_ref, xb_ref):
    @pl.when(pl.program_id(1) == 0)
    def _():
        xb_ref[...] = x_ref[...].astype(BF16)

    o_ref[...] = _dot(xb_ref[...], w_ref[...]).astype(o_ref.dtype)


def _matmul(x, w, out_dtype, tm=1024, tn=1024):
    t, k = x.shape
    n = w.shape[1]
    tm, tn = _tile(t, tm), _tile(n, tn)
    return pl.pallas_call(
        _matmul_kernel,
        grid=(t // tm, n // tn),
        in_specs=[pl.BlockSpec((tm, k), lambda i, j: (i, 0)),
                  pl.BlockSpec((k, tn), lambda i, j: (0, j))],
        out_specs=pl.BlockSpec((tm, tn), lambda i, j: (i, j)),
        out_shape=jax.ShapeDtypeStruct((t, n), out_dtype),
        scratch_shapes=[pltpu.VMEM((tm, k), BF16)],
        compiler_params=_params("parallel", "arbitrary"),
        name="matmul",
    )(x, w)


def _gla_proj_kernel(x_ref, w_ref, wz_ref, o_ref, z_ref, xb_ref):
    @pl.when(pl.program_id(1) == 0)
    def _():
        xb = x_ref[...].astype(BF16)
        xb_ref[...] = xb
        z_ref[...] = _dot(xb, wz_ref[...])

    o_ref[...] = _dot(xb_ref[...], w_ref[...]).astype(o_ref.dtype)


def _gla_proj(x, w_main, w_z, tm=1024, tn=1024):
    t, k = x.shape
    n = w_main.shape[1]
    tm, tn = _tile(t, tm), _tile(n, tn)
    return pl.pallas_call(
        _gla_proj_kernel,
        grid=(t // tm, n // tn),
        in_specs=[pl.BlockSpec((tm, k), lambda i, j: (i, 0)),
                  pl.BlockSpec((k, tn), lambda i, j: (0, j)),
                  pl.BlockSpec((k, LANES), lambda i, j: (0, 0))],
        out_specs=[pl.BlockSpec((tm, tn), lambda i, j: (i, j)),
                   pl.BlockSpec((tm, LANES), lambda i, j: (i, 0))],
        out_shape=[jax.ShapeDtypeStruct((t, n), BF16),
                   jax.ShapeDtypeStruct((t, LANES), F32)],
        scratch_shapes=[pltpu.VMEM((tm, k), BF16)],
        compiler_params=_params("parallel", "arbitrary"),
        name="gla_proj",
    )(x, w_main, w_z)


def _log_sigmoid(x):
    return jnp.minimum(x, 0.0) - jnp.log1p(jnp.exp(-jnp.abs(x)))


def _gla_kernel(*refs, reverse, nchunks, dk):
    if reverse:
        (q_ref, k_ref, v_ref, z_ref, wup_ref, bg_ref, of_ref, r_ref, ng_ref,
         o_ref, st_ref, g_ref) = refs
    else:
        q_ref, k_ref, v_ref, z_ref, wup_ref, bg_ref, o_ref, st_ref, g_ref = refs

    @pl.when(pl.program_id(2) == 0)
    def _():
        st_ref[...] = jnp.zeros_like(st_ref)

    pre = _dot(z_ref[...].astype(BF16), wup_ref[...]) + bg_ref[...]
    g_ref[...] = _log_sigmoid(pre) * (1.0 / GATE_NORM)

    ri = lax.broadcasted_iota(jnp.int32, (CHUNK, CHUNK), 0)
    ci = lax.broadcasted_iota(jnp.int32, (CHUNK, CHUNK), 1)
    if reverse:
        cum_mask, att_mask = ci >= ri, ci > ri
    else:
        cum_mask, att_mask = ci <= ri, ci <= ri
    tri = jnp.where(cum_mask, 1.0, 0.0).astype(BF16)
    scale = dk ** -0.5

    def body(c, carry):
        cc = (nchunks - 1 - c) if reverse else c
        rows = pl.ds(pl.multiple_of(cc * CHUNK, CHUNK), CHUNK)
        q = q_ref[rows, :].astype(F32) * scale
        k = k_ref[rows, :].astype(F32)
        v = v_ref[rows, :]
        g = g_ref[rows, :]
        g_hi = g.astype(BF16)
        g_lo = (g - g_hi.astype(F32)).astype(BF16)
        bc = _dot(tri, g_hi) + _dot(tri, g_lo)
        blast = bc[0:1, :] if reverse else bc[CHUNK - 1:CHUNK, :]
        qe = (q * jnp.exp(bc)).astype(BF16)
        ke = (k * jnp.exp(-bc)).astype(BF16)
        kd = (k * jnp.exp(blast - bc)).astype(BF16)
        a = jnp.where(att_mask, _dot_nt(qe, ke), 0.0).astype(BF16)
        st = st_ref[...]
        o = _dot(a, v) + _dot_nt(qe, st.astype(BF16))
        st_ref[...] = st * jnp.exp(blast) + _dot_tn(v, kd)
        if reverse:
            o = o + of_ref[rows, :]
            o = o * lax.rsqrt(jnp.mean(o * o, axis=-1, keepdims=True) + EPS) * ng_ref[...]
            r = r_ref[rows, :].astype(F32)
            o = o * (r / (1.0 + jnp.exp(-r)))
        o_ref[rows, :] = o.astype(o_ref.dtype)
        return carry

    lax.fori_loop(0, nchunks, body, 0)


def _gla_scan(proj, z, wup, bg, ng, o_fwd, batch, seq, d_model, reverse, rows=512):
    dk = d_model // 2 // GLA_HEADS
    dv = d_model // GLA_HEADS
    rows = _tile(seq, rows)
    nb = seq // rows
    kq, kk = 0, d_model // 2 // dk
    kv, kr = d_model // dv, 2 * d_model // dv
    d = 1 if reverse else 0

    def row_idx(b, i):
        return b * nb + ((nb - 1 - i) if reverse else i)

    in_specs = [
        pl.BlockSpec((rows, dk), lambda b, h, i: (row_idx(b, i), kq + h)),
        pl.BlockSpec((rows, dk), lambda b, h, i: (row_idx(b, i), kk + h)),
        pl.BlockSpec((rows, dv), lambda b, h, i: (row_idx(b, i), kv + h)),
        pl.BlockSpec((rows, LANES), lambda b, h, i: (row_idx(b, i), 0)),
        pl.BlockSpec((None, LANES, dk), lambda b, h, i: (d, 0, h)),
        pl.BlockSpec((None, 1, dk), lambda b, h, i: (d, 0, h)),
    ]
    args = [proj, proj, proj, z, wup, bg]
    if reverse:
        in_specs += [
            pl.BlockSpec((rows, dv), lambda b, h, i: (row_idx(b, i), h)),
            pl.BlockSpec((rows, dv), lambda b, h, i: (row_idx(b, i), kr + h)),
            pl.BlockSpec((1, dv), lambda b, h, i: (0, 0)),
        ]
        args += [o_fwd, proj, ng]
    return pl.pallas_call(
        functools.partial(_gla_kernel, reverse=reverse, nchunks=rows // CHUNK, dk=dk),
        grid=(batch, GLA_HEADS, nb),
        in_specs=in_specs,
        out_specs=pl.BlockSpec((rows, dv), lambda b, h, i: (row_idx(b, i), h)),
        out_shape=jax.ShapeDtypeStruct((batch * seq, d_model), BF16 if reverse else F32),
        scratch_shapes=[pltpu.VMEM((dv, dk), F32), pltpu.VMEM((rows, dk), F32)],
        compiler_params=_params("parallel", "parallel", "arbitrary"),
        name="gla_bwd" if reverse else "gla_fwd",
    )(*args)


def _proj_ln_kernel(a_ref, w_ref, x_ref, g_ref, b_ref, o_ref):
    y = DN_ALPHA * x_ref[...] + _dot(a_ref[...], w_ref[...])
    o_ref[...] = _layer_norm(y, g_ref[...], b_ref[...])


def _proj_ln(a, w, x, g, b, tm=512):
    t, k = a.shape
    d = w.shape[1]
    tm = _tile(t, tm)
    return pl.pallas_call(
        _proj_ln_kernel,
        grid=(t // tm,),
        in_specs=[pl.BlockSpec((tm, k), lambda i: (i, 0)),
                  pl.BlockSpec((k, d), lambda i: (0, 0)),
                  pl.BlockSpec((tm, d), lambda i: (i, 0)),
                  pl.BlockSpec((1, d), lambda i: (0, 0)),
                  pl.BlockSpec((1, d), lambda i: (0, 0))],
        out_specs=pl.BlockSpec((tm, d), lambda i: (i, 0)),
        out_shape=jax.ShapeDtypeStruct((t, d), F32),
        compiler_params=_params("parallel"),
        name="proj_ln",
    )(a, w, x, g, b)


def _xattn_kernel(q_ref, k_ref, v_ref, o_ref, *, dh):
    scale = dh ** -0.5
    for h in range(X_HEADS):
        cols = slice(h * dh, (h + 1) * dh)
        s = _dot_nt(q_ref[:, cols], k_ref[:, cols]) * scale
        p = jnp.exp(s - jnp.max(s, axis=-1, keepdims=True))
        l = jnp.sum(p, axis=-1, keepdims=True)
        o = _dot(p.astype(BF16), v_ref[:, cols]) / l
        o_ref[:, cols] = o.astype(o_ref.dtype)


def _xattn(q, kv, batch, seq, n_mem, d_model, tq=512):
    tq = _tile(seq, tq)
    nq = seq // tq
    return pl.pallas_call(
        functools.partial(_xattn_kernel, dh=d_model // X_HEADS),
        grid=(batch, nq),
        in_specs=[pl.BlockSpec((tq, d_model), lambda b, i: (b * nq + i, 0)),
                  pl.BlockSpec((n_mem, d_model), lambda b, i: (b, 0)),
                  pl.BlockSpec((n_mem, d_model), lambda b, i: (b, 1))],
        out_specs=pl.BlockSpec((tq, d_model), lambda b, i: (b * nq + i, 0)),
        out_shape=jax.ShapeDtypeStruct((batch * seq, d_model), BF16),
        compiler_params=_params("parallel", "parallel"),
        name="xattn",
    )(q, kv, kv)


def _mlp_kernel(x_ref, w1_ref, w2_ref, g_ref, b_ref, o_ref, xb_ref, acc_ref):
    j = pl.program_id(1)

    @pl.when(j == 0)
    def _():
        xb_ref[...] = x_ref[...].astype(BF16)
        acc_ref[...] = jnp.zeros_like(acc_ref)

    h = jnp.maximum(_dot(xb_ref[...], w1_ref[...]), 0.0)
    acc_ref[...] += _dot((h * h).astype(BF16), w2_ref[...])

    @pl.when(j == pl.num_programs(1) - 1)
    def _():
        y = DN_ALPHA * x_ref[...] + acc_ref[...]
        o_ref[...] = _layer_norm(y, g_ref[...], b_ref[...])


def _mlp(x, w1, w2, g, b, tm=512, tf=512):
    t, d = x.shape
    f = w1.shape[1]
    tm, tf = _tile(t, tm), _tile(f, tf)
    return pl.pallas_call(
        _mlp_kernel,
        grid=(t // tm, f // tf),
        in_specs=[pl.BlockSpec((tm, d), lambda i, j: (i, 0)),
                  pl.BlockSpec((d, tf), lambda i, j: (0, j)),
                  pl.BlockSpec((tf, d), lambda i, j: (j, 0)),
                  pl.BlockSpec((1, d), lambda i, j: (0, 0)),
                  pl.BlockSpec((1, d), lambda i, j: (0, 0))],
        out_specs=pl.BlockSpec((tm, d), lambda i, j: (i, 0)),
        out_shape=jax.ShapeDtypeStruct((t, d), F32),
        scratch_shapes=[pltpu.VMEM((tm, d), BF16), pltpu.VMEM((tm, d), F32)],
        compiler_params=_params("parallel", "arbitrary"),
        name="mlp",
    )(x, w1, w2, g, b)


def _qkv_rope_kernel(x_ref, w_ref, gain_ref, cos_ref, sin_ref, o_ref, xb_ref, *, n_rope_tiles):
    j = pl.program_id(1)

    @pl.when(j == 0)
    def _():
        xb_ref[...] = x_ref[...].astype(BF16)

    y = _dot(xb_ref[...], w_ref[...])

    @pl.when(j < n_rope_tiles)
    def _():
        cos = cos_ref[...]
        sin = sin_ref[...]
        lane = lax.broadcasted_iota(jnp.int32, (1, HEAD_DIM), 1)
        first = (lane % (HEAD_DIM // 2)) < (HEAD_DIM // 4)
        for h in range(y.shape[1] // HEAD_DIM):
            cols = slice(h * HEAD_DIM, (h + 1) * HEAD_DIM)
            t = y[:, cols]
            t = t * lax.rsqrt(jnp.mean(t * t, axis=-1, keepdims=True) + EPS) * gain_ref[:, cols]
            up = pltpu.roll(t, HEAD_DIM - HEAD_DIM // 4, 1)
            dn = pltpu.roll(t, HEAD_DIM // 4, 1)
            o_ref[:, cols] = (t * cos + jnp.where(first, up, dn) * sin).astype(o_ref.dtype)

    @pl.when(j >= n_rope_tiles)
    def _():
        o_ref[...] = y.astype(o_ref.dtype)


def _qkv_rope(x, w, gains, cos, sin, seq, n_rope_tiles, tm=512, tn=512):
    t, k = x.shape
    n = w.shape[1]
    tm, tn = _tile(seq, tm), _tile(n, tn)
    ns = seq // tm
    return pl.pallas_call(
        functools.partial(_qkv_rope_kernel, n_rope_tiles=n_rope_tiles),
        grid=(t // tm, n // tn),
        in_specs=[pl.BlockSpec((tm, k), lambda i, j: (i, 0)),
                  pl.BlockSpec((k, tn), lambda i, j: (0, j)),
                  pl.BlockSpec((1, tn), lambda i, j: (0, j)),
                  pl.BlockSpec((tm, HEAD_DIM), lambda i, j: (i % ns, 0)),
                  pl.BlockSpec((tm, HEAD_DIM), lambda i, j: (i % ns, 0))],
        out_specs=pl.BlockSpec((tm, tn), lambda i, j: (i, j)),
        out_shape=jax.ShapeDtypeStruct((t, n), BF16),
        scratch_shapes=[pltpu.VMEM((tm, k), BF16)],
        compiler_params=_params("parallel", "arbitrary"),
        name="qkv_rope",
    )(x, w, gains, cos, sin)


def _gqa_kernel(q_ref, k_ref, v_ref, o_ref, qs_ref, m_ref, l_ref, acc_ref, *, kb):
    tq = q_ref.shape[0]
    for g in range(GQA_GROUP):
        qs_ref[g * tq:(g + 1) * tq, :] = q_ref[:, g * HEAD_DIM:(g + 1) * HEAD_DIM]
    m_ref[...] = jnp.full_like(m_ref, -jnp.inf)
    l_ref[...] = jnp.zeros_like(l_ref)
    acc_ref[...] = jnp.zeros_like(acc_ref)

    def body(j, carry):
        ks = pl.ds(pl.multiple_of(j * kb, kb), kb)
        s = _dot_nt(qs_ref[...], k_ref[ks, :])
        m_prev = m_ref[...]
        m_new = jnp.maximum(m_prev, jnp.max(s, axis=-1, keepdims=True))
        alpha = jnp.exp(m_prev - m_new)
        p = jnp.exp(s - m_new)
        l_ref[...] = alpha * l_ref[...] + jnp.sum(p, axis=-1, keepdims=True)
        acc_ref[...] = alpha * acc_ref[...] + _dot(p.astype(BF16), v_ref[ks, :])
        m_ref[...] = m_new
        return carry

    lax.fori_loop(0, k_ref.shape[0] // kb, body, 0)
    out = acc_ref[...] / l_ref[...]
    for g in range(GQA_GROUP):
        o_ref[:, g * HEAD_DIM:(g + 1) * HEAD_DIM] = out[g * tq:(g + 1) * tq, :].astype(o_ref.dtype)


def _gqa(qkv, batch, seq, d_model, tq=256, kb=1024):
    n_kv = d_model // HEAD_DIM // GQA_GROUP
    tq, kb = _tile(seq, tq), _tile(seq, kb)
    nq = seq // tq
    gw = GQA_GROUP * HEAD_DIM
    k0 = d_model // HEAD_DIM
    v0 = k0 + n_kv
    rows = GQA_GROUP * tq
    return pl.pallas_call(
        functools.partial(_gqa_kernel, kb=kb),
        grid=(batch, n_kv, nq),
        in_specs=[pl.BlockSpec((tq, gw), lambda b, h, i: (b * nq + i, h)),
                  pl.BlockSpec((seq, HEAD_DIM), lambda b, h, i: (b, k0 + h)),
                  pl.BlockSpec((seq, HEAD_DIM), lambda b, h, i: (b, v0 + h))],
        out_specs=pl.BlockSpec((tq, gw), lambda b, h, i: (b * nq + i, h)),
        out_shape=jax.ShapeDtypeStruct((batch * seq, d_model), BF16),
        scratch_shapes=[pltpu.VMEM((rows, HEAD_DIM), BF16),
                        pltpu.VMEM((rows, 1), F32),
                        pltpu.VMEM((rows, 1), F32),
                        pltpu.VMEM((rows, HEAD_DIM), F32)],
        compiler_params=_params("parallel", "parallel", "arbitrary"),
        name="gqa",
    )(qkv, qkv, qkv)


def _rope_tables(n):
    axis_dim = HEAD_DIM // 2
    rows = n // GRID_W
    row = jnp.repeat(jnp.arange(rows, dtype=F32), GRID_W)
    col = jnp.tile(jnp.arange(GRID_W, dtype=F32), rows)
    inv = ROPE_THETA ** (-jnp.arange(0, axis_dim, 2, dtype=F32) / axis_dim)
    ar = row[:, None] * inv
    ac = col[:, None] * inv
    ang = jnp.concatenate([ar, ar, ac, ac], axis=-1)
    lane = jnp.arange(HEAD_DIM)
    sign = jnp.where((lane % axis_dim) < axis_dim // 2, -1.0, 1.0).astype(F32)
    return jnp.cos(ang), jnp.sin(ang) * sign


def _prep_weights(gla_w_in, gla_w_gate_up, gla_b_gate, gla_norm_g, gla_w_out,
                  att_w_qkv, att_q_gain, att_k_gain, att_w_out,
                  mem_w_q, mem_w_kv, mem_w_o, mlp_w1, mlp_w2, ln_g, ln_b):
    d_model = gla_w_in.shape[1]
    n_main = 3 * d_model
    dk_all = d_model // 2
    p = {}
    p["gla_w_main"] = gla_w_in[:, :, :n_main].astype(BF16)
    wz = gla_w_in[:, :, n_main:]
    p["gla_w_z"] = jnp.pad(wz, ((0, 0), (0, 0), (0, LANES - wz.shape[-1]))).astype(BF16)
    n_gla = gla_w_gate_up.shape[0]
    wup = jnp.zeros((n_gla, 2, LANES, dk_all), F32)
    wup = wup.at[:, 0, :GATE_RANK].set(gla_w_gate_up[:, 0])
    wup = wup.at[:, 1, GATE_RANK:2 * GATE_RANK].set(gla_w_gate_up[:, 1])
    p["gla_wup"] = wup.astype(BF16)
    p["gla_bg"] = gla_b_gate.reshape(n_gla, 2, 1, dk_all)
    p["gla_ng"] = gla_norm_g.reshape(n_gla, 1, -1)
    p["gla_w_out"] = gla_w_out.astype(BF16)
    p["att_w_qkv"] = att_w_qkv.astype(BF16)
    n_q = d_model // HEAD_DIM
    n_kv = n_q // GQA_GROUP
    gains = jnp.concatenate([
        jnp.tile(att_q_gain * (HEAD_DIM ** -0.5), (1, n_q)),
        jnp.tile(att_k_gain, (1, n_kv)),
        jnp.ones((att_q_gain.shape[0], n_kv * HEAD_DIM), F32)], axis=-1)
    p["att_gains"] = gains[:, None, :]
    p["att_w_out"] = att_w_out.astype(BF16)
    p["mem_w_q"] = mem_w_q.astype(BF16)
    p["mem_w_kv"] = mem_w_kv.astype(BF16)
    p["mem_w_o"] = mem_w_o.astype(BF16)
    p["mlp_w1"] = mlp_w1.astype(BF16)
    p["mlp_w2"] = mlp_w2.astype(BF16)
    p["ln_g"] = ln_g[:, :, None, :]
    p["ln_b"] = ln_b[:, :, None, :]
    return p


def _trunk(x3, mem3, p):
    batch, seq, d_model = x3.shape
    n_mem = mem3.shape[1]
    x = x3.reshape(batch * seq, d_model)
    mem = mem3.reshape(batch * n_mem, d_model)
    cos, sin = _rope_tables(seq)
    n_kv_cols = d_model // GQA_GROUP
    for i in range(DEPTH):
        j = i // 2
        if i % 2 == 0:
            proj, z = _gla_proj(x, p["gla_w_main"][j], p["gla_w_z"][j])
            o_f = _gla_scan(proj, z, p["gla_wup"][j], p["gla_bg"][j], p["gla_ng"][j], None,
                            batch, seq, d_model, reverse=False)
            h = _gla_scan(proj, z, p["gla_wup"][j], p["gla_bg"][j], p["gla_ng"][j], o_f,
                          batch, seq, d_model, reverse=True)
            w_out = p["gla_w_out"][j]
        else:
            tn = 512
            qkv = _qkv_rope(x, p["att_w_qkv"][j], p["att_gains"][j], cos, sin, seq,
                            n_rope_tiles=(d_model + n_kv_cols) // tn, tn=tn)
            h = _gqa(qkv, batch, seq, d_model)
            w_out = p["att_w_out"][j]
        x = _proj_ln(h, w_out, x, p["ln_g"][i, 0], p["ln_b"][i, 0])
        q = _matmul(x, p["mem_w_q"][i], BF16)
        kv = _matmul(mem, p["mem_w_kv"][i], BF16)
        a = _xattn(q, kv, batch, seq, n_mem, d_model)
        x = _proj_ln(a, p["mem_w_o"][i], x, p["ln_g"][i, 1], p["ln_b"][i, 1])
        x = _mlp(x, p["mlp_w1"][i], p["mlp_w2"][i], p["ln_g"][i, 2], p["ln_b"][i, 2])
    return x.reshape(batch, seq, d_model)


def kernel(x_prompt, x_sample, mem_prompt, mem_sample, gla_w_in, gla_w_gate_up, gla_b_gate, gla_norm_g, gla_w_out, att_w_qkv, att_q_gain, att_k_gain, att_w_out, mem_w_q, mem_w_kv, mem_w_o, mlp_w1, mlp_w2, ln_g, ln_b):
    p = _prep_weights(gla_w_in, gla_w_gate_up, gla_b_gate, gla_norm_g, gla_w_out,
                      att_w_qkv, att_q_gain, att_k_gain, att_w_out,
                      mem_w_q, mem_w_kv, mem_w_o, mlp_w1, mlp_w2, ln_g, ln_b)
    return (_trunk(x_prompt, mem_prompt, p), _trunk(x_sample, mem_sample, p))
```

```python
import functools
import math

import jax
import jax.numpy as jnp
from jax import lax
from jax.experimental import pallas as pl
from jax.experimental.pallas import tpu as pltpu

F32 = jnp.float32
BF16 = jnp.bfloat16

DEPTH = 2
GRID_W = 64
X_HEADS = 4
GLA_HEADS = 4
GATE_RANK = 16
GATE_NORM = 16.0
CHUNK = 64
HEAD_DIM = 128
GQA_GROUP = 4
ROPE_THETA = 10000.0
DN_ALPHA = (2.0 * DEPTH) ** 0.25
EPS = 1e-5

LANES = 128
VMEM_LIMIT = 56 * 1024 * 1024


def _params(*sem):
    return pltpu.CompilerParams(dimension_semantics=sem, vmem_limit_bytes=VMEM_LIMIT)


def _dot(a, b):
    return jnp.dot(a, b, preferred_element_type=F32)


def _dot_nt(a, b):
    return lax.dot_general(a, b, (((1,), (1,)), ((), ())), preferred_element_type=F32)


def _dot_tn(a, b):
    return lax.dot_general(a, b, (((0,), (0,)), ((), ())), preferred_element_type=F32)


def _layer_norm(y, g, b):
    mu = jnp.mean(y, axis=-1, keepdims=True)
    d = y - mu
    var = jnp.mean(d * d, axis=-1, keepdims=True)
    return d * lax.rsqrt(var + EPS) * g + b


def _tile(n, want):
    t = min(n, want)
    assert n % t == 0, (n, t)
    return t


def _matmul_kernel(x_ref, w_ref, o_ref, xb_ref):
    @pl.when(pl.program_id(1) == 0)
    def _():
        xb_ref[...] = x_ref[...].astype(BF16)

    o_ref[...] = _dot(xb_ref[...], w_ref[...]).astype(o_ref.dtype)


def _matmul(x, w, out_dtype, tm=1024, tn=1024):
    t, k = x.shape
    n = w.shape[1]
    tm, tn = _tile(t, tm), _tile(n, tn)
    return pl.pallas_call(
        _matmul_kernel,
        grid=(t // tm, n // tn),
        in_specs=[pl.BlockSpec((tm, k), lambda i, j: (i, 0)),
                  pl.BlockSpec((k, tn), lambda i, j: (0, j))],
        out_specs=pl.BlockSpec((tm, tn), lambda i, j: (i, j)),
        out_shape=jax.ShapeDtypeStruct((t, n), out_dtype),
        scratch_shapes=[pltpu.VMEM((tm, k), BF16)],
        compiler_params=_params("parallel", "arbitrary"),
        name="matmul",
    )(x, w)


def _gla_proj_kernel(x_ref, w_ref, wz_ref, o_ref, z_ref, xb_ref):
    @pl.when(pl.program_id(1) == 0)
    def _():
        xb = x_ref[...].astype(BF16)
        xb_ref[...] = xb
        z_ref[...] = _dot(xb, wz_ref[...])

    o_ref[...] = _dot(xb_ref[...], w_ref[...]).astype(o_ref.dtype)


def _gla_proj(x, w_main, w_z, tm=1024, tn=1024):
    t, k = x.shape
    n = w_main.shape[1]
    tm, tn = _tile(t, tm), _tile(n, tn)
    return pl.pallas_call(
        _gla_proj_kernel,
        grid=(t // tm, n // tn),
        in_specs=[pl.BlockSpec((tm, k), lambda i, j: (i, 0)),
                  pl.BlockSpec((k, tn), lambda i, j: (0, j)),
                  pl.BlockSpec((k, LANES), lambda i, j: (0, 0))],
        out_specs=[pl.BlockSpec((tm, tn), lambda i, j: (i, j)),
                   pl.BlockSpec((tm, LANES), lambda i, j: (i, 0))],
        out_shape=[jax.ShapeDtypeStruct((t, n), BF16),
                   jax.ShapeDtypeStruct((t, LANES), F32)],
        scratch_shapes=[pltpu.VMEM((tm, k), BF16)],
        compiler_params=_params("parallel", "arbitrary"),
        name="gla_proj",
    )(x, w_main, w_z)


GLA_HEADS_PER_STEP = 2


def _log_sigmoid(x):
    return jnp.minimum(x, 0.0) - jnp.log1p(jnp.exp(-jnp.abs(x)))


def _gla_kernel(qf_ref, kf_ref, vf_ref, zf_ref, qb_ref, kb_ref, vb_ref, zb_ref, wup_ref, bg_ref,
                of_ref, ob_ref, st_ref, g_ref, *, nchunks, dk, dv):
    hp = GLA_HEADS_PER_STEP
    dirs = ((qf_ref, kf_ref, vf_ref, zf_ref, of_ref), (qb_ref, kb_ref, vb_ref, zb_ref, ob_ref))

    @pl.when(pl.program_id(2) == 0)
    def _():
        st_ref[...] = jnp.zeros_like(st_ref)

    for d in range(2):
        pre = _dot(dirs[d][3][...].astype(BF16), wup_ref[d]) + bg_ref[d]
        g_ref[d] = _log_sigmoid(pre) * (1.0 / GATE_NORM)

    ri = lax.broadcasted_iota(jnp.int32, (CHUNK, CHUNK), 0)
    ci = lax.broadcasted_iota(jnp.int32, (CHUNK, CHUNK), 1)
    cum_masks = (ci <= ri, ci >= ri)
    att_masks = (ci <= ri, ci > ri)
    tris = [jnp.where(m, 1.0, 0.0).astype(BF16) for m in cum_masks]
    scale = dk ** -0.5

    def chunk_step(d, hh, c):
        q_ref, k_ref, v_ref, _, o_ref = dirs[d]
        cc = (nchunks - 1 - c) if d else c
        rows = pl.ds(pl.multiple_of(cc * CHUNK, CHUNK), CHUNK)
        kcols = slice(hh * dk, (hh + 1) * dk)
        vcols = slice(hh * dv, (hh + 1) * dv)
        q = q_ref[rows, kcols].astype(F32) * scale
        k = k_ref[rows, kcols].astype(F32)
        v = v_ref[rows, vcols]
        g = g_ref[d, rows, kcols]
        g_hi = g.astype(BF16)
        g_lo = (g - g_hi.astype(F32)).astype(BF16)
        bc = _dot(tris[d], g_hi) + _dot(tris[d], g_lo)
        blast = bc[0:1, :] if d else bc[CHUNK - 1:CHUNK, :]
        qe = (q * jnp.exp(bc)).astype(BF16)
        ke = (k * jnp.exp(-bc)).astype(BF16)
        kd = (k * jnp.exp(blast - bc)).astype(BF16)
        a = jnp.where(att_masks[d], _dot_nt(qe, ke), 0.0).astype(BF16)
        st = st_ref[d * hp + hh]
        o_ref[rows, vcols] = _dot(a, v) + _dot_nt(qe, st.astype(BF16))
        st_ref[d * hp + hh] = st * jnp.exp(blast) + _dot_tn(v, kd)

    def body(c, carry):
        for d in range(2):
            for hh in range(hp):
                chunk_step(d, hh, c)
        return carry

    lax.fori_loop(0, nchunks, body, 0, unroll=2)


def _gla_scan(proj, z, wup, bg, batch, seq, d_model, rows=512):
    dk = d_model // 2 // GLA_HEADS
    dv = d_model // GLA_HEADS
    hp = GLA_HEADS_PER_STEP
    rows = _tile(seq, rows)
    nb = seq // rows
    kk = d_model // 2 // (hp * dk)
    kv = d_model // (hp * dv)

    def fwd(b, i):
        return b * nb + i

    def bwd(b, i):
        return b * nb + nb - 1 - i

    def dir_specs(row):
        return [pl.BlockSpec((rows, hp * dk), lambda b, h, i: (row(b, i), h)),
                pl.BlockSpec((rows, hp * dk), lambda b, h, i: (row(b, i), kk + h)),
                pl.BlockSpec((rows, hp * dv), lambda b, h, i: (row(b, i), kv + h)),
                pl.BlockSpec((rows, LANES), lambda b, h, i: (row(b, i), 0))]

    out_sds = jax.ShapeDtypeStruct((batch * seq, d_model), F32)
    return pl.pallas_call(
        functools.partial(_gla_kernel, nchunks=rows // CHUNK, dk=dk, dv=dv),
        grid=(batch, GLA_HEADS // hp, nb),
        in_specs=dir_specs(fwd) + dir_specs(bwd) + [
            pl.BlockSpec((2, LANES, hp * dk), lambda b, h, i: (0, 0, h)),
            pl.BlockSpec((2, 1, hp * dk), lambda b, h, i: (0, 0, h))],
        out_specs=[pl.BlockSpec((rows, hp * dv), lambda b, h, i: (fwd(b, i), h)),
                   pl.BlockSpec((rows, hp * dv), lambda b, h, i: (bwd(b, i), h))],
        out_shape=[out_sds, out_sds],
        scratch_shapes=[pltpu.VMEM((2 * hp, dv, dk), F32), pltpu.VMEM((2, rows, hp * dk), F32)],
        compiler_params=_params("parallel", "parallel", "arbitrary"),
        name="gla_scan",
    )(proj, proj, proj, z, proj, proj, proj, z, wup, bg)


def _gla_out_kernel(of_ref, ob_ref, r_ref, ng_ref, w_ref, x_ref, g_ref, b_ref, o_ref, a_ref):
    dv = ng_ref.shape[1]
    for h in range(GLA_HEADS):
        cols = slice(h * dv, (h + 1) * dv)
        o = of_ref[:, cols] + ob_ref[:, cols]
        o = o * lax.rsqrt(jnp.mean(o * o, axis=-1, keepdims=True) + EPS) * ng_ref[...]
        r = r_ref[:, cols].astype(F32)
        a_ref[:, cols] = (o * (r / (1.0 + jnp.exp(-r)))).astype(BF16)
    y = DN_ALPHA * x_ref[...] + _dot(a_ref[...], w_ref[...])
    o_ref[...] = _layer_norm(y, g_ref[...], b_ref[...])


def _gla_out(o_f, o_b, proj, ng, w, x, g, b, tm=256):
    t, d = x.shape
    tm = _tile(t, tm)
    r_block = proj.shape[1] // d - 1
    row = lambda i: (i, 0)
    const = lambda i: (0, 0)
    return pl.pallas_call(
        _gla_out_kernel,
        grid=(t // tm,),
        in_specs=[pl.BlockSpec((tm, d), row),
                  pl.BlockSpec((tm, d), row),
                  pl.BlockSpec((tm, d), lambda i: (i, r_block)),
                  pl.BlockSpec(ng.shape, const),
                  pl.BlockSpec(w.shape, const),
                  pl.BlockSpec((tm, d), row),
                  pl.BlockSpec((1, d), const),
                  pl.BlockSpec((1, d), const)],
        out_specs=pl.BlockSpec((tm, d), row),
        out_shape=jax.ShapeDtypeStruct((t, d), F32),
        scratch_shapes=[pltpu.VMEM((tm, d), BF16)],
        compiler_params=_params("parallel"),
        name="gla_out",
    )(o_f, o_b, proj, ng, w, x, g, b)


def _proj_ln_kernel(a_ref, w_ref, x_ref, g_ref, b_ref, o_ref):
    y = DN_ALPHA * x_ref[...] + _dot(a_ref[...], w_ref[...])
    o_ref[...] = _layer_norm(y, g_ref[...], b_ref[...])


def _proj_ln(a, w, x, g, b, tm=512):
    t, k = a.shape
    d = w.shape[1]
    tm = _tile(t, tm)
    return pl.pallas_call(
        _proj_ln_kernel,
        grid=(t // tm,),
        in_specs=[pl.BlockSpec((tm, k), lambda i: (i, 0)),
                  pl.BlockSpec((k, d), lambda i: (0, 0)),
                  pl.BlockSpec((tm, d), lambda i: (i, 0)),
                  pl.BlockSpec((1, d), lambda i: (0, 0)),
                  pl.BlockSpec((1, d), lambda i: (0, 0))],
        out_specs=pl.BlockSpec((tm, d), lambda i: (i, 0)),
        out_shape=jax.ShapeDtypeStruct((t, d), F32),
        compiler_params=_params("parallel"),
        name="proj_ln",
    )(a, w, x, g, b)


def _xattn_kernel(q_ref, k_ref, v_ref, o_ref, *, dh):
    scale = dh ** -0.5
    for h in range(X_HEADS):
        cols = slice(h * dh, (h + 1) * dh)
        s = _dot_nt(q_ref[:, cols], k_ref[:, cols]) * scale
        p = jnp.exp(s - jnp.max(s, axis=-1, keepdims=True))
        l = jnp.sum(p, axis=-1, keepdims=True)
        o = _dot(p.astype(BF16), v_ref[:, cols]) / l
        o_ref[:, cols] = o.astype(o_ref.dtype)


def _xattn(q, kv, batch, seq, n_mem, d_model, tq=512):
    tq = _tile(seq, tq)
    nq = seq // tq
    return pl.pallas_call(
        functools.partial(_xattn_kernel, dh=d_model // X_HEADS),
        grid=(batch, nq),
        in_specs=[pl.BlockSpec((tq, d_model), lambda b, i: (b * nq + i, 0)),
                  pl.BlockSpec((n_mem, d_model), lambda b, i: (b, 0)),
                  pl.BlockSpec((n_mem, d_model), lambda b, i: (b, 1))],
        out_specs=pl.BlockSpec((tq, d_model), lambda b, i: (b * nq + i, 0)),
        out_shape=jax.ShapeDtypeStruct((batch * seq, d_model), BF16),
        compiler_params=_params("parallel", "parallel"),
        name="xattn",
    )(q, kv, kv)


def _mlp_kernel(x_ref, w1_ref, w2_ref, g_ref, b_ref, o_ref, xb_ref):
    j = pl.program_id(1)

    @pl.when(j == 0)
    def _():
        xb_ref[...] = x_ref[...].astype(BF16)
        o_ref[...] = jnp.zeros_like(o_ref)

    h = jnp.maximum(_dot(xb_ref[...], w1_ref[...]), 0.0)
    o_ref[...] += _dot((h * h).astype(BF16), w2_ref[...])

    @pl.when(j == pl.num_programs(1) - 1)
    def _():
        y = DN_ALPHA * x_ref[...] + o_ref[...]
        o_ref[...] = _layer_norm(y, g_ref[...], b_ref[...])


def _mlp(x, w1, w2, g, b, tm=1024, tf=512):
    t, d = x.shape
    f = w1.shape[1]
    tm, tf = _tile(t, tm), _tile(f, tf)
    return pl.pallas_call(
        _mlp_kernel,
        grid=(t // tm, f // tf),
        in_specs=[pl.BlockSpec((tm, d), lambda i, j: (i, 0)),
                  pl.BlockSpec((d, tf), lambda i, j: (0, j)),
                  pl.BlockSpec((tf, d), lambda i, j: (j, 0)),
                  pl.BlockSpec((1, d), lambda i, j: (0, 0)),
                  pl.BlockSpec((1, d), lambda i, j: (0, 0))],
        out_specs=pl.BlockSpec((tm, d), lambda i, j: (i, 0)),
        out_shape=jax.ShapeDtypeStruct((t, d), F32),
        scratch_shapes=[pltpu.VMEM((tm, d), BF16)],
        compiler_params=_params("parallel", "arbitrary"),
        name="mlp",
    )(x, w1, w2, g, b)


def _qkv_rope_kernel(x_ref, w_ref, gain_ref, cos_ref, sin_ref, o_ref, xb_ref, *, n_rope_tiles):
    j = pl.program_id(1)

    @pl.when(j == 0)
    def _():
        xb_ref[...] = x_ref[...].astype(BF16)

    y = _dot(xb_ref[...], w_ref[...])

    @pl.when(j < n_rope_tiles)
    def _():
        cos = cos_ref[...]
        sin = sin_ref[...]
        lane = lax.broadcasted_iota(jnp.int32, (1, HEAD_DIM), 1)
        first = (lane % (HEAD_DIM // 2)) < (HEAD_DIM // 4)
        for h in range(y.shape[1] // HEAD_DIM):
            cols = slice(h * HEAD_DIM, (h + 1) * HEAD_DIM)
            t = y[:, cols]
            t = t * lax.rsqrt(jnp.mean(t * t, axis=-1, keepdims=True) + EPS) * gain_ref[:, cols]
            up = pltpu.roll(t, HEAD_DIM - HEAD_DIM // 4, 1)
            dn = pltpu.roll(t, HEAD_DIM // 4, 1)
            o_ref[:, cols] = (t * cos + jnp.where(first, up, dn) * sin).astype(o_ref.dtype)

    @pl.when(j >= n_rope_tiles)
    def _():
        o_ref[...] = y.astype(o_ref.dtype)


def _qkv_rope(x, w, gains, cos, sin, seq, n_rope_tiles, tm=512, tn=512):
    t, k = x.shape
    n = w.shape[1]
    tm, tn = _tile(seq, tm), _tile(n, tn)
    ns = seq // tm
    return pl.pallas_call(
        functools.partial(_qkv_rope_kernel, n_rope_tiles=n_rope_tiles),
        grid=(t // tm, n // tn),
        in_specs=[pl.BlockSpec((tm, k), lambda i, j: (i, 0)),
                  pl.BlockSpec((k, tn), lambda i, j: (0, j)),
                  pl.BlockSpec((1, tn), lambda i, j: (0, j)),
                  pl.BlockSpec((tm, HEAD_DIM), lambda i, j: (i % ns, 0)),
                  pl.BlockSpec((tm, HEAD_DIM), lambda i, j: (i % ns, 0))],
        out_specs=pl.BlockSpec((tm, tn), lambda i, j: (i, j)),
        out_shape=jax.ShapeDtypeStruct((t, n), BF16),
        scratch_shapes=[pltpu.VMEM((tm, k), BF16)],
        compiler_params=_params("parallel", "arbitrary"),
        name="qkv_rope",
    )(x, w, gains, cos, sin)


SAFE_LOG2 = 60.0


def _gqa_kernel(safe_ref, q_ref, k_ref, v_ref, o_ref, vx_ref, m_ref, acc_ref, *, kb):
    nkb = k_ref.shape[0] // kb

    @pl.when(pl.program_id(2) == 0)
    def _():
        vx_ref[:, :HEAD_DIM] = v_ref[...]
        vx_ref[:, HEAD_DIM:] = jnp.ones_like(v_ref)

    safe = safe_ref[0] != 0

    @pl.when(safe)
    def _():
        def block(j, first):
            ks = pl.ds(pl.multiple_of(j * kb, kb), kb)
            k_blk = k_ref[ks, :]
            vx_blk = vx_ref[ks, :]
            for g in range(GQA_GROUP):
                s = _dot_nt(q_ref[:, g * HEAD_DIM:(g + 1) * HEAD_DIM], k_blk)
                pv = _dot(jnp.exp2(s).astype(BF16), vx_blk)
                acc_ref[g] = pv if first else acc_ref[g] + pv

        block(0, True)

        def body(j, carry):
            block(j, False)
            return carry

        lax.fori_loop(1, nkb, body, 0)

    @pl.when(jnp.logical_not(safe))
    def _():
        m_ref[...] = jnp.full_like(m_ref, -jnp.inf)
        acc_ref[...] = jnp.zeros_like(acc_ref)

        def body(j, carry):
            ks = pl.ds(pl.multiple_of(j * kb, kb), kb)
            k_blk = k_ref[ks, :]
            vx_blk = vx_ref[ks, :]
            for g in range(GQA_GROUP):
                s = _dot_nt(q_ref[:, g * HEAD_DIM:(g + 1) * HEAD_DIM], k_blk)
                m_prev = m_ref[g]
                m_new = jnp.maximum(m_prev, jnp.max(s, axis=-1, keepdims=True))
                alpha = jnp.exp2(m_prev - m_new)
                p = jnp.exp2(s - jnp.tile(m_new, (1, kb // LANES)))
                acc_ref[g] = jnp.tile(alpha, (1, 2)) * acc_ref[g] + _dot(p.astype(BF16), vx_blk)
                m_ref[g] = m_new
            return carry

        lax.fori_loop(0, nkb, body, 0)

    for g in range(GQA_GROUP):
        acc = acc_ref[g]
        o_ref[:, g * HEAD_DIM:(g + 1) * HEAD_DIM] = (acc[:, :HEAD_DIM] / acc[:, HEAD_DIM:]).astype(o_ref.dtype)


def _gqa(safe, qkv, batch, seq, d_model, tq=256, kb=2048):
    n_kv = d_model // HEAD_DIM // GQA_GROUP
    tq, kb = _tile(seq, tq), _tile(seq, kb)
    nq = seq // tq
    gw = GQA_GROUP * HEAD_DIM
    k0 = d_model // HEAD_DIM
    v0 = k0 + n_kv
    scratch = [pltpu.VMEM((seq, 2 * HEAD_DIM), BF16),
               pltpu.VMEM((GQA_GROUP, tq, HEAD_DIM), F32),
               pltpu.VMEM((GQA_GROUP, tq, 2 * HEAD_DIM), F32)]
    return pl.pallas_call(
        functools.partial(_gqa_kernel, kb=kb),
        grid=(batch, n_kv, nq),
        in_specs=[pl.BlockSpec(memory_space=pltpu.SMEM),
                  pl.BlockSpec((tq, gw), lambda b, h, i: (b * nq + i, h)),
                  pl.BlockSpec((seq, HEAD_DIM), lambda b, h, i: (b, k0 + h)),
                  pl.BlockSpec((seq, HEAD_DIM), lambda b, h, i: (b, v0 + h))],
        out_specs=pl.BlockSpec((tq, gw), lambda b, h, i: (b * nq + i, h)),
        out_shape=jax.ShapeDtypeStruct((batch * seq, d_model), BF16),
        scratch_shapes=scratch,
        compiler_params=_params("parallel", "parallel", "arbitrary"),
        name="gqa",
    )(safe, qkv, qkv, qkv)


def _gqa_safe_flag(q_gain_scaled, k_gain):
    bound = 1.02 * HEAD_DIM * jnp.max(jnp.abs(q_gain_scaled), axis=-1) * jnp.max(jnp.abs(k_gain), axis=-1)
    return (bound <= SAFE_LOG2).astype(jnp.int32)[:, None]


def _rope_tables(n):
    axis_dim = HEAD_DIM // 2
    rows = n // GRID_W
    row = jnp.repeat(jnp.arange(rows, dtype=F32), GRID_W)
    col = jnp.tile(jnp.arange(GRID_W, dtype=F32), rows)
    inv = ROPE_THETA ** (-jnp.arange(0, axis_dim, 2, dtype=F32) / axis_dim)
    ar = row[:, None] * inv
    ac = col[:, None] * inv
    ang = jnp.concatenate([ar, ar, ac, ac], axis=-1)
    lane = jnp.arange(HEAD_DIM)
    sign = jnp.where((lane % axis_dim) < axis_dim // 2, -1.0, 1.0).astype(F32)
    return jnp.cos(ang), jnp.sin(ang) * sign


def _prep_weights(gla_w_in, gla_w_gate_up, gla_b_gate, gla_norm_g, gla_w_out,
                  att_w_qkv, att_q_gain, att_k_gain, att_w_out,
                  mem_w_q, mem_w_kv, mem_w_o, mlp_w1, mlp_w2, ln_g, ln_b):
    d_model = gla_w_in.shape[1]
    n_main = 3 * d_model
    dk_all = d_model // 2
    p = {}
    p["gla_w_main"] = gla_w_in[:, :, :n_main].astype(BF16)
    wz = gla_w_in[:, :, n_main:]
    p["gla_w_z"] = jnp.pad(wz, ((0, 0), (0, 0), (0, LANES - wz.shape[-1]))).astype(BF16)
    n_gla = gla_w_gate_up.shape[0]
    wup = jnp.zeros((n_gla, 2, LANES, dk_all), F32)
    wup = wup.at[:, 0, :GATE_RANK].set(gla_w_gate_up[:, 0])
    wup = wup.at[:, 1, GATE_RANK:2 * GATE_RANK].set(gla_w_gate_up[:, 1])
    p["gla_wup"] = wup.astype(BF16)
    p["gla_bg"] = gla_b_gate.reshape(n_gla, 2, 1, dk_all)
    p["gla_ng"] = gla_norm_g.reshape(n_gla, 1, -1)
    p["gla_w_out"] = gla_w_out.astype(BF16)
    p["att_w_qkv"] = att_w_qkv.astype(BF16)
    n_q = d_model // HEAD_DIM
    n_kv = n_q // GQA_GROUP
    q_gain = att_q_gain * (HEAD_DIM ** -0.5 * math.log2(math.e))
    p["att_safe"] = _gqa_safe_flag(q_gain, att_k_gain)
    gains = jnp.concatenate([
        jnp.tile(q_gain, (1, n_q)),
        jnp.tile(att_k_gain, (1, n_kv)),
        jnp.ones((att_q_gain.shape[0], n_kv * HEAD_DIM), F32)], axis=-1)
    p["att_gains"] = gains[:, None, :]
    p["att_w_out"] = att_w_out.astype(BF16)
    p["mem_w_q"] = mem_w_q.astype(BF16)
    p["mem_w_kv"] = mem_w_kv.astype(BF16)
    p["mem_w_o"] = mem_w_o.astype(BF16)
    p["mlp_w1"] = mlp_w1.astype(BF16)
    p["mlp_w2"] = mlp_w2.astype(BF16)
    p["ln_g"] = ln_g[:, :, None, :]
    p["ln_b"] = ln_b[:, :, None, :]
    return p


def _trunk(x3, mem3, p):
    batch, seq, d_model = x3.shape
    n_mem = mem3.shape[1]
    x = x3.reshape(batch * seq, d_model)
    mem = mem3.reshape(batch * n_mem, d_model)
    cos, sin = _rope_tables(seq)
    n_kv_cols = d_model // GQA_GROUP
    for i in range(DEPTH):
        j = i // 2
        if i % 2 == 0:
            proj, z = _gla_proj(x, p["gla_w_main"][j], p["gla_w_z"][j])
            o_f, o_b = _gla_scan(proj, z, p["gla_wup"][j], p["gla_bg"][j], batch, seq, d_model)
            x = _gla_out(o_f, o_b, proj, p["gla_ng"][j], p["gla_w_out"][j], x,
                         p["ln_g"][i, 0], p["ln_b"][i, 0])
        else:
            tn = 512
            qkv = _qkv_rope(x, p["att_w_qkv"][j], p["att_gains"][j], cos, sin, seq,
                            n_rope_tiles=(d_model + n_kv_cols) // tn, tn=tn)
            h = _gqa(p["att_safe"][j], qkv, batch, seq, d_model)
            x = _proj_ln(h, p["att_w_out"][j], x, p["ln_g"][i, 0], p["ln_b"][i, 0])
        q = _matmul(x, p["mem_w_q"][i], BF16)
        kv = _matmul(mem, p["mem_w_kv"][i], BF16)
        a = _xattn(q, kv, batch, seq, n_mem, d_model)
        x = _proj_ln(a, p["mem_w_o"][i], x, p["ln_g"][i, 1], p["ln_b"][i, 1])
        x = _mlp(x, p["mlp_w1"][i], p["mlp_w2"][i], p["ln_g"][i, 2], p["ln_b"][i, 2])
    return x.reshape(batch, seq, d_model)


def kernel(x_prompt, x_sample, mem_prompt, mem_sample, gla_w_in, gla_w_gate_up, gla_b_gate, gla_norm_g, gla_w_out, att_w_qkv, att_q_gain, att_k_gain, att_w_out, mem_w_q, mem_w_kv, mem_w_o, mlp_w1, mlp_w2, ln_g, ln_b):
    p = _prep_weights(gla_w_in, gla_w_gate_up, gla_b_gate, gla_norm_g, gla_w_out,
                      att_w_qkv, att_q_gain, att_k_gain, att_w_out,
                      mem_w_q, mem_w_kv, mem_w_o, mlp_w1, mlp_w2, ln_g, ln_b)
    return (_trunk(x_prompt, mem_prompt, p), _trunk(x_sample, mem_sample, p))
```

```python
import functools
import math

import jax
import jax.numpy as jnp
from jax import lax
from jax.experimental import pallas as pl
from jax.experimental.pallas import tpu as pltpu

F32 = jnp.float32
BF16 = jnp.bfloat16

DEPTH = 2
GRID_W = 64
X_HEADS = 4
GLA_HEADS = 4
GATE_RANK = 16
GATE_NORM = 16.0
CHUNK = 64
HEAD_DIM = 128
GQA_GROUP = 4
ROPE_THETA = 10000.0
DN_ALPHA = (2.0 * DEPTH) ** 0.25
EPS = 1e-5

LANES = 128
VMEM_LIMIT = 56 * 1024 * 1024


def _params(*sem):
    return pltpu.CompilerParams(dimension_semantics=sem, vmem_limit_bytes=VMEM_LIMIT)


def _dot(a, b):
    return jnp.dot(a, b, preferred_element_type=F32)


def _dot_nt(a, b):
    return lax.dot_general(a, b, (((1,), (1,)), ((), ())), preferred_element_type=F32)


def _dot_tn(a, b):
    return lax.dot_general(a, b, (((0,), (0,)), ((), ())), preferred_element_type=F32)


def _layer_norm(y, g, b):
    mu = jnp.mean(y, axis=-1, keepdims=True)
    d = y - mu
    var = jnp.mean(d * d, axis=-1, keepdims=True)
    return d * lax.rsqrt(var + EPS) * g + b


def _tile(n, want):
    t = min(n, want)
    assert n % t == 0, (n, t)
    return t


def _matmul_kernel(x_ref, w_ref, o_ref, xb_ref):
    @pl.when(pl.program_id(1) == 0)
    def _():
        xb_ref[...] = x_ref[...].astype(BF16)

    o_ref[...] = _dot(xb_ref[...], w_ref[...]).astype(o_ref.dtype)


def _matmul(x, w, out_dtype, tm=1024, tn=1024):
    t, k = x.shape
    n = w.shape[1]
    tm, tn = _tile(t, tm), _tile(n, tn)
    return pl.pallas_call(
        _matmul_kernel,
        grid=(t // tm, n // tn),
        in_specs=[pl.BlockSpec((tm, k), lambda i, j: (i, 0)),
                  pl.BlockSpec((k, tn), lambda i, j: (0, j))],
        out_specs=pl.BlockSpec((tm, tn), lambda i, j: (i, j)),
        out_shape=jax.ShapeDtypeStruct((t, n), out_dtype),
        scratch_shapes=[pltpu.VMEM((tm, k), BF16)],
        compiler_params=_params("parallel", "arbitrary"),
        name="matmul",
    )(x, w)


def _gla_proj_kernel(x_ref, w_ref, wz_ref, o_ref, z_ref, xb_ref):
    @pl.when(pl.program_id(1) == 0)
    def _():
        xb = x_ref[...].astype(BF16)
        xb_ref[...] = xb
        z_ref[...] = _dot(xb, wz_ref[...])

    o_ref[...] = _dot(xb_ref[...], w_ref[...]).astype(o_ref.dtype)


def _gla_proj(x, w_main, w_z, tm=1024, tn=1024):
    t, k = x.shape
    n = w_main.shape[1]
    tm, tn = _tile(t, tm), _tile(n, tn)
    return pl.pallas_call(
        _gla_proj_kernel,
        grid=(t // tm, n // tn),
        in_specs=[pl.BlockSpec((tm, k), lambda i, j: (i, 0)),
                  pl.BlockSpec((k, tn), lambda i, j: (0, j)),
                  pl.BlockSpec((k, LANES), lambda i, j: (0, 0))],
        out_specs=[pl.BlockSpec((tm, tn), lambda i, j: (i, j)),
                   pl.BlockSpec((tm, LANES), lambda i, j: (i, 0))],
        out_shape=[jax.ShapeDtypeStruct((t, n), BF16),
                   jax.ShapeDtypeStruct((t, LANES), F32)],
        scratch_shapes=[pltpu.VMEM((tm, k), BF16)],
        compiler_params=_params("parallel", "arbitrary"),
        name="gla_proj",
    )(x, w_main, w_z)


GLA_HEADS_PER_STEP = 2
GLA_CHUNKS_PER_TILE = 4


def _log_sigmoid(x):
    return jnp.minimum(x, 0.0) - jnp.log1p(jnp.exp(-jnp.abs(x)))


def _gla_kernel(qf_ref, kf_ref, vf_ref, zf_ref, qb_ref, kb_ref, vb_ref, zb_ref, wup_ref, bg_ref,
                of_ref, ob_ref, st_ref, g_ref, qe_ref, kd_ref, dec_ref, *, nchunks, dk, dv):
    hp = GLA_HEADS_PER_STEP
    dirs = ((qf_ref, kf_ref, vf_ref, zf_ref, of_ref), (qb_ref, kb_ref, vb_ref, zb_ref, ob_ref))

    @pl.when(pl.program_id(2) == 0)
    def _():
        st_ref[...] = jnp.zeros_like(st_ref)

    for d in range(2):
        pre = _dot(dirs[d][3][...].astype(BF16), wup_ref[d]) + bg_ref[d]
        g_ref[d] = _log_sigmoid(pre) * (1.0 / GATE_NORM)

    tile = GLA_CHUNKS_PER_TILE * CHUNK
    ri = lax.broadcasted_iota(jnp.int32, (tile, tile), 0)
    ci = lax.broadcasted_iota(jnp.int32, (tile, tile), 1)
    same = (ri // CHUNK) == (ci // CHUNK)
    cum_masks = (same & (ci <= ri), same & (ci >= ri))
    att_masks = (same & (ci <= ri), same & (ci > ri))
    tris = [jnp.where(m, 1.0, 0.0).astype(BF16) for m in cum_masks]
    scale = dk ** -0.5

    for d in range(2):
        q_ref, k_ref, v_ref, _, o_ref = dirs[d]
        for hh in range(hp):
            kcols = slice(hh * dk, (hh + 1) * dk)
            vcols = slice(hh * dv, (hh + 1) * dv)
            for t in range(nchunks // GLA_CHUNKS_PER_TILE):
                rows = slice(t * tile, (t + 1) * tile)
                g = g_ref[d, rows, kcols]
                g_hi = g.astype(BF16)
                g_lo = (g - g_hi.astype(F32)).astype(BF16)
                bc = _dot(tris[d], g_hi) + _dot(tris[d], g_lo)
                tot = jnp.concatenate(
                    [jnp.broadcast_to(bc[r:r + 1, :], (CHUNK, dk))
                     for r in range(0 if d else CHUNK - 1, tile, CHUNK)], axis=0)
                q = q_ref[rows, kcols].astype(F32) * scale
                k = k_ref[rows, kcols].astype(F32)
                qe = (q * jnp.exp(bc)).astype(BF16)
                ke = (k * jnp.exp(-bc)).astype(BF16)
                a = jnp.where(att_masks[d], _dot_nt(qe, ke), 0.0).astype(BF16)
                o_ref[rows, vcols] = _dot(a, v_ref[rows, vcols])
                qe_ref[d * hp + hh, rows, :] = qe
                kd_ref[d * hp + hh, rows, :] = (k * jnp.exp(tot - bc)).astype(BF16)
                dec_ref[d * hp + hh, rows, :] = jnp.exp(tot)

    def chunk_step(d, hh, c):
        _, _, v_ref, _, o_ref = dirs[d]
        idx = d * hp + hh
        cc = (nchunks - 1 - c) if d else c
        rows = pl.ds(pl.multiple_of(cc * CHUNK, CHUNK), CHUNK)
        vcols = slice(hh * dv, (hh + 1) * dv)
        st = st_ref[idx]
        o_ref[rows, vcols] += _dot_nt(qe_ref[idx, rows, :], st.astype(BF16))
        dec = dec_ref[idx, pl.ds(pl.multiple_of(cc * CHUNK, CHUNK), 1), :]
        st_ref[idx] = st * dec + _dot_tn(v_ref[rows, vcols], kd_ref[idx, rows, :])

    def body(c, carry):
        for d in range(2):
            for hh in range(hp):
                chunk_step(d, hh, c)
        return carry

    lax.fori_loop(0, nchunks, body, 0, unroll=2)


def _gla_scan(proj, z, wup, bg, batch, seq, d_model, rows=512):
    dk = d_model // 2 // GLA_HEADS
    dv = d_model // GLA_HEADS
    hp = GLA_HEADS_PER_STEP
    rows = _tile(seq, rows)
    nb = seq // rows
    kk = d_model // 2 // (hp * dk)
    kv = d_model // (hp * dv)

    def fwd(b, i):
        return b * nb + i

    def bwd(b, i):
        return b * nb + nb - 1 - i

    def dir_specs(row):
        return [pl.BlockSpec((rows, hp * dk), lambda b, h, i: (row(b, i), h)),
                pl.BlockSpec((rows, hp * dk), lambda b, h, i: (row(b, i), kk + h)),
                pl.BlockSpec((rows, hp * dv), lambda b, h, i: (row(b, i), kv + h)),
                pl.BlockSpec((rows, LANES), lambda b, h, i: (row(b, i), 0))]

    out_sds = jax.ShapeDtypeStruct((batch * seq, d_model), F32)
    return pl.pallas_call(
        functools.partial(_gla_kernel, nchunks=rows // CHUNK, dk=dk, dv=dv),
        grid=(batch, GLA_HEADS // hp, nb),
        in_specs=dir_specs(fwd) + dir_specs(bwd) + [
            pl.BlockSpec((2, LANES, hp * dk), lambda b, h, i: (0, 0, h)),
            pl.BlockSpec((2, 1, hp * dk), lambda b, h, i: (0, 0, h))],
        out_specs=[pl.BlockSpec((rows, hp * dv), lambda b, h, i: (fwd(b, i), h)),
                   pl.BlockSpec((rows, hp * dv), lambda b, h, i: (bwd(b, i), h))],
        out_shape=[out_sds, out_sds],
        scratch_shapes=[pltpu.VMEM((2 * hp, dv, dk), F32),
                        pltpu.VMEM((2, rows, hp * dk), F32),
                        pltpu.VMEM((2 * hp, rows, dk), BF16),
                        pltpu.VMEM((2 * hp, rows, dk), BF16),
                        pltpu.VMEM((2 * hp, rows, dk), F32)],
        compiler_params=_params("parallel", "parallel", "arbitrary"),
        name="gla_scan",
    )(proj, proj, proj, z, proj, proj, proj, z, wup, bg)


def _gla_out_kernel(of_ref, ob_ref, r_ref, ng_ref, w_ref, x_ref, g_ref, b_ref, o_ref, a_ref):
    dv = ng_ref.shape[1]
    for h in range(GLA_HEADS):
        cols = slice(h * dv, (h + 1) * dv)
        o = of_ref[:, cols] + ob_ref[:, cols]
        o = o * lax.rsqrt(jnp.mean(o * o, axis=-1, keepdims=True) + EPS) * ng_ref[...]
        r = r_ref[:, cols].astype(F32)
        a_ref[:, cols] = (o * (r / (1.0 + jnp.exp(-r)))).astype(BF16)
    y = DN_ALPHA * x_ref[...] + _dot(a_ref[...], w_ref[...])
    o_ref[...] = _layer_norm(y, g_ref[...], b_ref[...])


def _gla_out(o_f, o_b, proj, ng, w, x, g, b, tm=256):
    t, d = x.shape
    tm = _tile(t, tm)
    r_block = proj.shape[1] // d - 1
    row = lambda i: (i, 0)
    const = lambda i: (0, 0)
    return pl.pallas_call(
        _gla_out_kernel,
        grid=(t // tm,),
        in_specs=[pl.BlockSpec((tm, d), row),
                  pl.BlockSpec((tm, d), row),
                  pl.BlockSpec((tm, d), lambda i: (i, r_block)),
                  pl.BlockSpec(ng.shape, const),
                  pl.BlockSpec(w.shape, const),
                  pl.BlockSpec((tm, d), row),
                  pl.BlockSpec((1, d), const),
                  pl.BlockSpec((1, d), const)],
        out_specs=pl.BlockSpec((tm, d), row),
        out_shape=jax.ShapeDtypeStruct((t, d), F32),
        scratch_shapes=[pltpu.VMEM((tm, d), BF16)],
        compiler_params=_params("parallel"),
        name="gla_out",
    )(o_f, o_b, proj, ng, w, x, g, b)


def _proj_ln_kernel(a_ref, w_ref, x_ref, g_ref, b_ref, o_ref):
    y = DN_ALPHA * x_ref[...] + _dot(a_ref[...], w_ref[...])
    o_ref[...] = _layer_norm(y, g_ref[...], b_ref[...])


def _proj_ln(a, w, x, g, b, tm=512):
    t, k = a.shape
    d = w.shape[1]
    tm = _tile(t, tm)
    return pl.pallas_call(
        _proj_ln_kernel,
        grid=(t // tm,),
        in_specs=[pl.BlockSpec((tm, k), lambda i: (i, 0)),
                  pl.BlockSpec((k, d), lambda i: (0, 0)),
                  pl.BlockSpec((tm, d), lambda i: (i, 0)),
                  pl.BlockSpec((1, d), lambda i: (0, 0)),
                  pl.BlockSpec((1, d), lambda i: (0, 0))],
        out_specs=pl.BlockSpec((tm, d), lambda i: (i, 0)),
        out_shape=jax.ShapeDtypeStruct((t, d), F32),
        compiler_params=_params("parallel"),
        name="proj_ln",
    )(a, w, x, g, b)


def _xattn_kernel(q_ref, k_ref, v_ref, o_ref, *, dh):
    scale = dh ** -0.5
    for h in range(X_HEADS):
        cols = slice(h * dh, (h + 1) * dh)
        s = _dot_nt(q_ref[:, cols], k_ref[:, cols]) * scale
        p = jnp.exp(s - jnp.max(s, axis=-1, keepdims=True))
        l = jnp.sum(p, axis=-1, keepdims=True)
        o = _dot(p.astype(BF16), v_ref[:, cols]) / l
        o_ref[:, cols] = o.astype(o_ref.dtype)


def _xattn(q, kv, batch, seq, n_mem, d_model, tq=512):
    tq = _tile(seq, tq)
    nq = seq // tq
    return pl.pallas_call(
        functools.partial(_xattn_kernel, dh=d_model // X_HEADS),
        grid=(batch, nq),
        in_specs=[pl.BlockSpec((tq, d_model), lambda b, i: (b * nq + i, 0)),
                  pl.BlockSpec((n_mem, d_model), lambda b, i: (b, 0)),
                  pl.BlockSpec((n_mem, d_model), lambda b, i: (b, 1))],
        out_specs=pl.BlockSpec((tq, d_model), lambda b, i: (b * nq + i, 0)),
        out_shape=jax.ShapeDtypeStruct((batch * seq, d_model), BF16),
        compiler_params=_params("parallel", "parallel"),
        name="xattn",
    )(q, kv, kv)


def _mlp_kernel(x_ref, w1_ref, w2_ref, g_ref, b_ref, o_ref, xb_ref):
    j = pl.program_id(1)

    @pl.when(j == 0)
    def _():
        xb_ref[...] = x_ref[...].astype(BF16)
        o_ref[...] = jnp.zeros_like(o_ref)

    h = jnp.maximum(_dot(xb_ref[...], w1_ref[...]), 0.0)
    o_ref[...] += _dot((h * h).astype(BF16), w2_ref[...])

    @pl.when(j == pl.num_programs(1) - 1)
    def _():
        y = DN_ALPHA * x_ref[...] + o_ref[...]
        o_ref[...] = _layer_norm(y, g_ref[...], b_ref[...])


def _mlp(x, w1, w2, g, b, tm=1024, tf=512):
    t, d = x.shape
    f = w1.shape[1]
    tm, tf = _tile(t, tm), _tile(f, tf)
    return pl.pallas_call(
        _mlp_kernel,
        grid=(t // tm, f // tf),
        in_specs=[pl.BlockSpec((tm, d), lambda i, j: (i, 0)),
                  pl.BlockSpec((d, tf), lambda i, j: (0, j)),
                  pl.BlockSpec((tf, d), lambda i, j: (j, 0)),
                  pl.BlockSpec((1, d), lambda i, j: (0, 0)),
                  pl.BlockSpec((1, d), lambda i, j: (0, 0))],
        out_specs=pl.BlockSpec((tm, d), lambda i, j: (i, 0)),
        out_shape=jax.ShapeDtypeStruct((t, d), F32),
        scratch_shapes=[pltpu.VMEM((tm, d), BF16)],
        compiler_params=_params("parallel", "arbitrary"),
        name="mlp",
    )(x, w1, w2, g, b)


def _qk_rope_kernel(x_ref, w_ref, gain_ref, cos_ref, sin_ref, o_ref, xb_ref):
    @pl.when(pl.program_id(1) == 0)
    def _():
        xb_ref[...] = x_ref[...].astype(BF16)

    pair = 2 * HEAD_DIM
    cos = jnp.tile(cos_ref[...], (1, 2))
    sin = jnp.tile(sin_ref[...], (1, 2))
    src = lax.broadcasted_iota(jnp.int32, (pair, pair), 0)
    dst = lax.broadcasted_iota(jnp.int32, (pair, pair), 1)
    quarter = HEAD_DIM // 4
    partner = jnp.where((dst % (2 * quarter)) < quarter, dst + quarter, dst - quarter)
    perm = jnp.where(src == partner, 1.0, 0.0).astype(BF16)
    y = _dot(xb_ref[...], w_ref[...])
    for c in range(o_ref.shape[1] // pair):
        heads = []
        for h in range(2):
            cols = slice(c * pair + h * HEAD_DIM, c * pair + (h + 1) * HEAD_DIM)
            t = y[:, cols]
            heads.append(t * lax.rsqrt(jnp.mean(t * t, axis=-1, keepdims=True) + EPS) * gain_ref[:, cols])
        t = jnp.concatenate(heads, axis=-1)
        t_hi = t.astype(BF16)
        t_lo = (t - t_hi.astype(F32)).astype(BF16)
        rot = _dot(t_hi, perm) + _dot(t_lo, perm)
        o_ref[:, c * pair:(c + 1) * pair] = (t * cos + rot * sin).astype(o_ref.dtype)


def _qk_rope(x, w, gains, cos, sin, seq, tm=512, tn=512):
    t, k = x.shape
    n = w.shape[1]
    tm, tn = _tile(seq, tm), _tile(n, tn)
    ns = seq // tm
    return pl.pallas_call(
        _qk_rope_kernel,
        grid=(t // tm, n // tn),
        in_specs=[pl.BlockSpec((tm, k), lambda i, j: (i, 0)),
                  pl.BlockSpec((k, tn), lambda i, j: (0, j)),
                  pl.BlockSpec((1, tn), lambda i, j: (0, j)),
                  pl.BlockSpec((tm, HEAD_DIM), lambda i, j: (i % ns, 0)),
                  pl.BlockSpec((tm, HEAD_DIM), lambda i, j: (i % ns, 0))],
        out_specs=pl.BlockSpec((tm, tn), lambda i, j: (i, j)),
        out_shape=jax.ShapeDtypeStruct((t, n), BF16),
        scratch_shapes=[pltpu.VMEM((tm, k), BF16)],
        compiler_params=_params("parallel", "arbitrary"),
        name="qk_rope",
    )(x, w, gains, cos, sin)


SAFE_LOG2 = 60.0


def _gqa_kernel(safe_ref, q_ref, k_ref, v_ref, o_ref, vx_ref, m_ref, acc_ref, *, kb):
    nkb = k_ref.shape[0] // kb

    @pl.when(pl.program_id(2) == 0)
    def _():
        vx_ref[:, :HEAD_DIM] = v_ref[...]
        vx_ref[:, HEAD_DIM:] = jnp.ones_like(v_ref)

    safe = safe_ref[0] != 0

    @pl.when(safe)
    def _():
        def block(j, first):
            ks = pl.ds(pl.multiple_of(j * kb, kb), kb)
            k_blk = k_ref[ks, :]
            vx_blk = vx_ref[ks, :]
            for g in range(GQA_GROUP):
                s = _dot_nt(q_ref[:, g * HEAD_DIM:(g + 1) * HEAD_DIM], k_blk)
                pv = _dot(jnp.exp2(s).astype(BF16), vx_blk)
                acc_ref[g] = pv if first else acc_ref[g] + pv

        block(0, True)

        def body(j, carry):
            block(j, False)
            return carry

        lax.fori_loop(1, nkb, body, 0)

    @pl.when(jnp.logical_not(safe))
    def _():
        m_ref[...] = jnp.full_like(m_ref, -jnp.inf)
        acc_ref[...] = jnp.zeros_like(acc_ref)

        def body(j, carry):
            ks = pl.ds(pl.multiple_of(j * kb, kb), kb)
            k_blk = k_ref[ks, :]
            vx_blk = vx_ref[ks, :]
            for g in range(GQA_GROUP):
                s = _dot_nt(q_ref[:, g * HEAD_DIM:(g + 1) * HEAD_DIM], k_blk)
                m_prev = m_ref[g]
                m_new = jnp.maximum(m_prev, jnp.max(s, axis=-1, keepdims=True))
                alpha = jnp.exp2(m_prev - m_new)
                p = jnp.exp2(s - jnp.tile(m_new, (1, kb // LANES)))
                acc_ref[g] = jnp.tile(alpha, (1, 2)) * acc_ref[g] + _dot(p.astype(BF16), vx_blk)
                m_ref[g] = m_new
            return carry

        lax.fori_loop(0, nkb, body, 0)

    for g in range(GQA_GROUP):
        acc = acc_ref[g]
        o_ref[:, g * HEAD_DIM:(g + 1) * HEAD_DIM] = (acc[:, :HEAD_DIM] / acc[:, HEAD_DIM:]).astype(o_ref.dtype)


def _gqa(safe, qk, v, batch, seq, d_model, tq=256, kb=2048):
    n_kv = d_model // HEAD_DIM // GQA_GROUP
    tq, kb = _tile(seq, tq), _tile(seq, kb)
    nq = seq // tq
    gw = GQA_GROUP * HEAD_DIM
    k0 = d_model // HEAD_DIM
    scratch = [pltpu.VMEM((seq, 2 * HEAD_DIM), BF16),
               pltpu.VMEM((GQA_GROUP, tq, HEAD_DIM), F32),
               pltpu.VMEM((GQA_GROUP, tq, 2 * HEAD_DIM), F32)]
    return pl.pallas_call(
        functools.partial(_gqa_kernel, kb=kb),
        grid=(batch, n_kv, nq),
        in_specs=[pl.BlockSpec(memory_space=pltpu.SMEM),
                  pl.BlockSpec((tq, gw), lambda b, h, i: (b * nq + i, h)),
                  pl.BlockSpec((seq, HEAD_DIM), lambda b, h, i: (b, k0 + h)),
                  pl.BlockSpec((seq, HEAD_DIM), lambda b, h, i: (b, h))],
        out_specs=pl.BlockSpec((tq, gw), lambda b, h, i: (b * nq + i, h)),
        out_shape=jax.ShapeDtypeStruct((batch * seq, d_model), BF16),
        scratch_shapes=scratch,
        compiler_params=_params("parallel", "parallel", "arbitrary"),
        name="gqa",
    )(safe, qk, qk, v)


def _gqa_safe_flag(q_gain_scaled, k_gain):
    bound = 1.02 * HEAD_DIM * jnp.max(jnp.abs(q_gain_scaled), axis=-1) * jnp.max(jnp.abs(k_gain), axis=-1)
    return (bound <= SAFE_LOG2).astype(jnp.int32)[:, None]


def _rope_tables(n):
    axis_dim = HEAD_DIM // 2
    rows = n // GRID_W
    row = jnp.repeat(jnp.arange(rows, dtype=F32), GRID_W)
    col = jnp.tile(jnp.arange(GRID_W, dtype=F32), rows)
    inv = ROPE_THETA ** (-jnp.arange(0, axis_dim, 2, dtype=F32) / axis_dim)
    ar = row[:, None] * inv
    ac = col[:, None] * inv
    ang = jnp.concatenate([ar, ar, ac, ac], axis=-1)
    lane = jnp.arange(HEAD_DIM)
    sign = jnp.where((lane % axis_dim) < axis_dim // 2, -1.0, 1.0).astype(F32)
    return jnp.cos(ang), jnp.sin(ang) * sign


def _prep_weights(gla_w_in, gla_w_gate_up, gla_b_gate, gla_norm_g, gla_w_out,
                  att_w_qkv, att_q_gain, att_k_gain, att_w_out,
                  mem_w_q, mem_w_kv, mem_w_o, mlp_w1, mlp_w2, ln_g, ln_b):
    d_model = gla_w_in.shape[1]
    n_main = 3 * d_model
    dk_all = d_model // 2
    p = {}
    p["gla_w_main"] = gla_w_in[:, :, :n_main].astype(BF16)
    wz = gla_w_in[:, :, n_main:]
    p["gla_w_z"] = jnp.pad(wz, ((0, 0), (0, 0), (0, LANES - wz.shape[-1]))).astype(BF16)
    n_gla = gla_w_gate_up.shape[0]
    wup = jnp.zeros((n_gla, 2, LANES, dk_all), F32)
    wup = wup.at[:, 0, :GATE_RANK].set(gla_w_gate_up[:, 0])
    wup = wup.at[:, 1, GATE_RANK:2 * GATE_RANK].set(gla_w_gate_up[:, 1])
    p["gla_wup"] = wup.astype(BF16)
    p["gla_bg"] = gla_b_gate.reshape(n_gla, 2, 1, dk_all)
    p["gla_ng"] = gla_norm_g.reshape(n_gla, 1, -1)
    p["gla_w_out"] = gla_w_out.astype(BF16)
    n_q = d_model // HEAD_DIM
    n_kv = n_q // GQA_GROUP
    n_qk = (n_q + n_kv) * HEAD_DIM
    p["att_w_qk"] = att_w_qkv[:, :, :n_qk].astype(BF16)
    p["att_w_v"] = att_w_qkv[:, :, n_qk:].astype(BF16)
    q_gain = att_q_gain * (HEAD_DIM ** -0.5 * math.log2(math.e))
    p["att_safe"] = _gqa_safe_flag(q_gain, att_k_gain)
    gains = jnp.concatenate([jnp.tile(q_gain, (1, n_q)), jnp.tile(att_k_gain, (1, n_kv))], axis=-1)
    p["att_gains"] = gains[:, None, :]
    p["att_w_out"] = att_w_out.astype(BF16)
    p["mem_w_q"] = mem_w_q.astype(BF16)
    p["mem_w_kv"] = mem_w_kv.astype(BF16)
    p["mem_w_o"] = mem_w_o.astype(BF16)
    p["mlp_w1"] = mlp_w1.astype(BF16)
    p["mlp_w2"] = mlp_w2.astype(BF16)
    p["ln_g"] = ln_g[:, :, None, :]
    p["ln_b"] = ln_b[:, :, None, :]
    return p


def _trunk(x3, mem3, p):
    batch, seq, d_model = x3.shape
    n_mem = mem3.shape[1]
    x = x3.reshape(batch * seq, d_model)
    mem = mem3.reshape(batch * n_mem, d_model)
    cos, sin = _rope_tables(seq)
    for i in range(DEPTH):
        j = i // 2
        if i % 2 == 0:
            proj, z = _gla_proj(x, p["gla_w_main"][j], p["gla_w_z"][j])
            o_f, o_b = _gla_scan(proj, z, p["gla_wup"][j], p["gla_bg"][j], batch, seq, d_model)
            x = _gla_out(o_f, o_b, proj, p["gla_ng"][j], p["gla_w_out"][j], x,
                         p["ln_g"][i, 0], p["ln_b"][i, 0])
        else:
            qk = _qk_rope(x, p["att_w_qk"][j], p["att_gains"][j], cos, sin, seq)
            v = _matmul(x, p["att_w_v"][j], BF16)
            h = _gqa(p["att_safe"][j], qk, v, batch, seq, d_model)
            x = _proj_ln(h, p["att_w_out"][j], x, p["ln_g"][i, 0], p["ln_b"][i, 0])
        q = _matmul(x, p["mem_w_q"][i], BF16)
        kv = _matmul(mem, p["mem_w_kv"][i], BF16)
        a = _xattn(q, kv, batch, seq, n_mem, d_model)
        x = _proj_ln(a, p["mem_w_o"][i], x, p["ln_g"][i, 1], p["ln_b"][i, 1])
        x = _mlp(x, p["mlp_w1"][i], p["mlp_w2"][i], p["ln_g"][i, 2], p["ln_b"][i, 2])
    return x.reshape(batch, seq, d_model)


def kernel(x_prompt, x_sample, mem_prompt, mem_sample, gla_w_in, gla_w_gate_up, gla_b_gate, gla_norm_g, gla_w_out, att_w_qkv, att_q_gain, att_k_gain, att_w_out, mem_w_q, mem_w_kv, mem_w_o, mlp_w1, mlp_w2, ln_g, ln_b):
    p = _prep_weights(gla_w_in, gla_w_gate_up, gla_b_gate, gla_norm_g, gla_w_out,
                      att_w_qkv, att_q_gain, att_k_gain, att_w_out,
                      mem_w_q, mem_w_kv, mem_w_o, mlp_w1, mlp_w2, ln_g, ln_b)
    return (_trunk(x_prompt, mem_prompt, p), _trunk(x_sample, mem_sample, p))
```

```python
import functools
import math

import jax
import jax.numpy as jnp
from jax import lax
from jax.experimental import pallas as pl
from jax.experimental.pallas import tpu as pltpu

F32 = jnp.float32
BF16 = jnp.bfloat16

DEPTH = 2
GRID_W = 64
X_HEADS = 4
GLA_HEADS = 4
GATE_RANK = 16
GATE_NORM = 16.0
CHUNK = 64
HEAD_DIM = 128
GQA_GROUP = 4
ROPE_THETA = 10000.0
DN_ALPHA = (2.0 * DEPTH) ** 0.25
EPS = 1e-5

LANES = 128
VMEM_LIMIT = 56 * 1024 * 1024


def _params(*sem):
    return pltpu.CompilerParams(dimension_semantics=sem, vmem_limit_bytes=VMEM_LIMIT)


def _dot(a, b):
    return jnp.dot(a, b, preferred_element_type=F32)


def _dot_nt(a, b):
    return lax.dot_general(a, b, (((1,), (1,)), ((), ())), preferred_element_type=F32)


def _dot_tn(a, b):
    return lax.dot_general(a, b, (((0,), (0,)), ((), ())), preferred_element_type=F32)


def _layer_norm(y, g, b):
    mu = jnp.mean(y, axis=-1, keepdims=True)
    d = y - mu
    var = jnp.mean(d * d, axis=-1, keepdims=True)
    return d * lax.rsqrt(var + EPS) * g + b


def _tile(n, want):
    t = min(n, want)
    assert n % t == 0, (n, t)
    return t


def _matmul_kernel(x_ref, w_ref, o_ref, xb_ref):
    @pl.when(pl.program_id(1) == 0)
    def _():
        xb_ref[...] = x_ref[...].astype(BF16)

    o_ref[...] = _dot(xb_ref[...], w_ref[...]).astype(o_ref.dtype)


def _matmul(x, w, out_dtype, tm=1024, tn=1024):
    t, k = x.shape
    n = w.shape[1]
    tm, tn = _tile(t, tm), _tile(n, tn)
    return pl.pallas_call(
        _matmul_kernel,
        grid=(t // tm, n // tn),
        in_specs=[pl.BlockSpec((tm, k), lambda i, j: (i, 0)),
                  pl.BlockSpec((k, tn), lambda i, j: (0, j))],
        out_specs=pl.BlockSpec((tm, tn), lambda i, j: (i, j)),
        out_shape=jax.ShapeDtypeStruct((t, n), out_dtype),
        scratch_shapes=[pltpu.VMEM((tm, k), BF16)],
        compiler_params=_params("parallel", "arbitrary"),
        name="matmul",
    )(x, w)


def _gla_proj_kernel(x_ref, w_ref, wz_ref, o_ref, z_ref, xb_ref):
    @pl.when(pl.program_id(1) == 0)
    def _():
        xb = x_ref[...].astype(BF16)
        xb_ref[...] = xb
        z_ref[...] = _dot(xb, wz_ref[...])

    o_ref[...] = _dot(xb_ref[...], w_ref[...]).astype(o_ref.dtype)


def _gla_proj(x, w_main, w_z, tm=1024, tn=1024):
    t, k = x.shape
    n = w_main.shape[1]
    tm, tn = _tile(t, tm), _tile(n, tn)
    return pl.pallas_call(
        _gla_proj_kernel,
        grid=(t // tm, n // tn),
        in_specs=[pl.BlockSpec((tm, k), lambda i, j: (i, 0)),
                  pl.BlockSpec((k, tn), lambda i, j: (0, j)),
                  pl.BlockSpec((k, LANES), lambda i, j: (0, 0))],
        out_specs=[pl.BlockSpec((tm, tn), lambda i, j: (i, j)),
                   pl.BlockSpec((tm, LANES), lambda i, j: (i, 0))],
        out_shape=[jax.ShapeDtypeStruct((t, n), BF16),
                   jax.ShapeDtypeStruct((t, LANES), F32)],
        scratch_shapes=[pltpu.VMEM((tm, k), BF16)],
        compiler_params=_params("parallel", "arbitrary"),
        name="gla_proj",
    )(x, w_main, w_z)


GLA_HEADS_PER_STEP = 2
GLA_CHUNKS_PER_TILE = 4


def _log_sigmoid(x):
    return jnp.minimum(x, 0.0) - jnp.log1p(jnp.exp(-jnp.abs(x)))


def _gla_kernel(qf_ref, kf_ref, vf_ref, zf_ref, qb_ref, kb_ref, vb_ref, zb_ref, wup_ref, bg_ref,
                of_ref, ob_ref, st_ref, g_ref, qe_ref, kd_ref, dec_ref, *, nchunks, dk, dv):
    hp = GLA_HEADS_PER_STEP
    dirs = ((qf_ref, kf_ref, vf_ref, zf_ref, of_ref), (qb_ref, kb_ref, vb_ref, zb_ref, ob_ref))

    @pl.when(pl.program_id(2) == 0)
    def _():
        st_ref[...] = jnp.zeros_like(st_ref)

    for d in range(2):
        pre = _dot(dirs[d][3][...].astype(BF16), wup_ref[d]) + bg_ref[d]
        g_ref[d] = _log_sigmoid(pre) * (1.0 / GATE_NORM)

    tile = GLA_CHUNKS_PER_TILE * CHUNK
    ri = lax.broadcasted_iota(jnp.int32, (tile, tile), 0)
    ci = lax.broadcasted_iota(jnp.int32, (tile, tile), 1)
    same = (ri // CHUNK) == (ci // CHUNK)
    cum_masks = (same & (ci <= ri), same & (ci >= ri))
    att_masks = (same & (ci <= ri), same & (ci > ri))
    tris = [jnp.where(m, 1.0, 0.0).astype(BF16) for m in cum_masks]
    scale = dk ** -0.5

    for d in range(2):
        q_ref, k_ref, v_ref, _, o_ref = dirs[d]
        for hh in range(hp):
            kcols = slice(hh * dk, (hh + 1) * dk)
            vcols = slice(hh * dv, (hh + 1) * dv)
            for t in range(nchunks // GLA_CHUNKS_PER_TILE):
                rows = slice(t * tile, (t + 1) * tile)
                g = g_ref[d, rows, kcols]
                g_hi = g.astype(BF16)
                g_lo = (g - g_hi.astype(F32)).astype(BF16)
                bc = _dot(tris[d], g_hi) + _dot(tris[d], g_lo)
                tot = jnp.concatenate(
                    [jnp.broadcast_to(bc[r:r + 1, :], (CHUNK, dk))
                     for r in range(0 if d else CHUNK - 1, tile, CHUNK)], axis=0)
                q = q_ref[rows, kcols].astype(F32) * scale
                k = k_ref[rows, kcols].astype(F32)
                qe = (q * jnp.exp(bc)).astype(BF16)
                ke = (k * jnp.exp(-bc)).astype(BF16)
                a = jnp.where(att_masks[d], _dot_nt(qe, ke), 0.0).astype(BF16)
                o_ref[rows, vcols] = _dot(a, v_ref[rows, vcols])
                qe_ref[d * hp + hh, rows, :] = qe
                kd_ref[d * hp + hh, rows, :] = (k * jnp.exp(tot - bc)).astype(BF16)
                dec_ref[d * hp + hh, rows, :] = jnp.exp(tot)

    def chunk_step(d, hh, c):
        _, _, v_ref, _, o_ref = dirs[d]
        idx = d * hp + hh
        cc = (nchunks - 1 - c) if d else c
        rows = pl.ds(pl.multiple_of(cc * CHUNK, CHUNK), CHUNK)
        vcols = slice(hh * dv, (hh + 1) * dv)
        st = st_ref[idx]
        o_ref[rows, vcols] += _dot_nt(qe_ref[idx, rows, :], st.astype(BF16))
        dec = dec_ref[idx, pl.ds(pl.multiple_of(cc * CHUNK, CHUNK), 1), :]
        st_ref[idx] = st * dec + _dot_tn(v_ref[rows, vcols], kd_ref[idx, rows, :])

    def body(c, carry):
        for d in range(2):
            for hh in range(hp):
                chunk_step(d, hh, c)
        return carry

    lax.fori_loop(0, nchunks, body, 0, unroll=2)


def _gla_scan(proj, z, wup, bg, batch, seq, d_model, rows=512):
    dk = d_model // 2 // GLA_HEADS
    dv = d_model // GLA_HEADS
    hp = GLA_HEADS_PER_STEP
    rows = _tile(seq, rows)
    nb = seq // rows
    kk = d_model // 2 // (hp * dk)
    kv = d_model // (hp * dv)

    def fwd(b, i):
        return b * nb + i

    def bwd(b, i):
        return b * nb + nb - 1 - i

    def dir_specs(row):
        return [pl.BlockSpec((rows, hp * dk), lambda b, h, i: (row(b, i), h)),
                pl.BlockSpec((rows, hp * dk), lambda b, h, i: (row(b, i), kk + h)),
                pl.BlockSpec((rows, hp * dv), lambda b, h, i: (row(b, i), kv + h)),
                pl.BlockSpec((rows, LANES), lambda b, h, i: (row(b, i), 0))]

    out_sds = jax.ShapeDtypeStruct((batch * seq, d_model), F32)
    return pl.pallas_call(
        functools.partial(_gla_kernel, nchunks=rows // CHUNK, dk=dk, dv=dv),
        grid=(batch, GLA_HEADS // hp, nb),
        in_specs=dir_specs(fwd) + dir_specs(bwd) + [
            pl.BlockSpec((2, LANES, hp * dk), lambda b, h, i: (0, 0, h)),
            pl.BlockSpec((2, 1, hp * dk), lambda b, h, i: (0, 0, h))],
        out_specs=[pl.BlockSpec((rows, hp * dv), lambda b, h, i: (fwd(b, i), h)),
                   pl.BlockSpec((rows, hp * dv), lambda b, h, i: (bwd(b, i), h))],
        out_shape=[out_sds, out_sds],
        scratch_shapes=[pltpu.VMEM((2 * hp, dv, dk), F32),
                        pltpu.VMEM((2, rows, hp * dk), F32),
                        pltpu.VMEM((2 * hp, rows, dk), BF16),
                        pltpu.VMEM((2 * hp, rows, dk), BF16),
                        pltpu.VMEM((2 * hp, rows, dk), F32)],
        compiler_params=_params("parallel", "parallel", "arbitrary"),
        name="gla_scan",
    )(proj, proj, proj, z, proj, proj, proj, z, wup, bg)


def _gla_out_kernel(of_ref, ob_ref, r_ref, ng_ref, w_ref, x_ref, g_ref, b_ref, o_ref, a_ref):
    dv = ng_ref.shape[1]
    for h in range(GLA_HEADS):
        cols = slice(h * dv, (h + 1) * dv)
        o = of_ref[:, cols] + ob_ref[:, cols]
        o = o * lax.rsqrt(jnp.mean(o * o, axis=-1, keepdims=True) + EPS) * ng_ref[...]
        r = r_ref[:, cols].astype(F32)
        a_ref[:, cols] = (o * (r / (1.0 + jnp.exp(-r)))).astype(BF16)
    y = DN_ALPHA * x_ref[...] + _dot(a_ref[...], w_ref[...])
    o_ref[...] = _layer_norm(y, g_ref[...], b_ref[...])


def _gla_out(o_f, o_b, proj, ng, w, x, g, b, tm=512):
    t, d = x.shape
    tm = _tile(t, tm)
    r_block = proj.shape[1] // d - 1
    row = lambda i: (i, 0)
    const = lambda i: (0, 0)
    return pl.pallas_call(
        _gla_out_kernel,
        grid=(t // tm,),
        in_specs=[pl.BlockSpec((tm, d), row),
                  pl.BlockSpec((tm, d), row),
                  pl.BlockSpec((tm, d), lambda i: (i, r_block)),
                  pl.BlockSpec(ng.shape, const),
                  pl.BlockSpec(w.shape, const, pipeline_mode=pl.Buffered(1)),
                  pl.BlockSpec((tm, d), row),
                  pl.BlockSpec((1, d), const),
                  pl.BlockSpec((1, d), const)],
        out_specs=pl.BlockSpec((tm, d), row),
        out_shape=jax.ShapeDtypeStruct((t, d), F32),
        scratch_shapes=[pltpu.VMEM((tm, d), BF16)],
        compiler_params=_params("parallel"),
        name="gla_out",
    )(o_f, o_b, proj, ng, w, x, g, b)


def _proj_ln_kernel(a_ref, w_ref, x_ref, g_ref, b_ref, o_ref):
    y = DN_ALPHA * x_ref[...] + _dot(a_ref[...], w_ref[...])
    o_ref[...] = _layer_norm(y, g_ref[...], b_ref[...])


def _proj_ln(a, w, x, g, b, tm=512):
    t, k = a.shape
    d = w.shape[1]
    tm = _tile(t, tm)
    return pl.pallas_call(
        _proj_ln_kernel,
        grid=(t // tm,),
        in_specs=[pl.BlockSpec((tm, k), lambda i: (i, 0)),
                  pl.BlockSpec((k, d), lambda i: (0, 0)),
                  pl.BlockSpec((tm, d), lambda i: (i, 0)),
                  pl.BlockSpec((1, d), lambda i: (0, 0)),
                  pl.BlockSpec((1, d), lambda i: (0, 0))],
        out_specs=pl.BlockSpec((tm, d), lambda i: (i, 0)),
        out_shape=jax.ShapeDtypeStruct((t, d), F32),
        compiler_params=_params("parallel"),
        name="proj_ln",
    )(a, w, x, g, b)


def _xattn_kernel(x_ref, k_ref, v_ref, wq_ref, wo_ref, g_ref, b_ref, o_ref, a_ref, *, dh):
    scale = dh ** -0.5
    xb = x_ref[...].astype(BF16)
    for h in range(X_HEADS):
        cols = slice(h * dh, (h + 1) * dh)
        q = _dot(xb, wq_ref[:, cols]).astype(BF16)
        s = _dot_nt(q, k_ref[:, cols]) * scale
        p = jnp.exp(s - jnp.max(s, axis=-1, keepdims=True))
        l = jnp.sum(p, axis=-1, keepdims=True)
        a_ref[:, cols] = (_dot(p.astype(BF16), v_ref[:, cols]) / l).astype(BF16)
    y = DN_ALPHA * x_ref[...] + _dot(a_ref[...], wo_ref[...])
    o_ref[...] = _layer_norm(y, g_ref[...], b_ref[...])


def _xattn(x, kv, w_q, w_o, g, b, batch, seq, n_mem, tq=512):
    d_model = x.shape[1]
    tq = _tile(seq, tq)
    nq = seq // tq
    row = lambda bi, i: (bi * nq + i, 0)
    const = lambda bi, i: (0, 0)
    return pl.pallas_call(
        functools.partial(_xattn_kernel, dh=d_model // X_HEADS),
        grid=(batch, nq),
        in_specs=[pl.BlockSpec((tq, d_model), row),
                  pl.BlockSpec((n_mem, d_model), lambda bi, i: (bi, 0)),
                  pl.BlockSpec((n_mem, d_model), lambda bi, i: (bi, 1)),
                  pl.BlockSpec(w_q.shape, const, pipeline_mode=pl.Buffered(1)),
                  pl.BlockSpec(w_o.shape, const, pipeline_mode=pl.Buffered(1)),
                  pl.BlockSpec((1, d_model), const),
                  pl.BlockSpec((1, d_model), const)],
        out_specs=pl.BlockSpec((tq, d_model), row),
        out_shape=jax.ShapeDtypeStruct((batch * seq, d_model), F32),
        scratch_shapes=[pltpu.VMEM((tq, d_model), BF16)],
        compiler_params=_params("parallel", "parallel"),
        name="xattn",
    )(x, kv, kv, w_q, w_o, g, b)


def _mlp_kernel(x_ref, w1_ref, w2_ref, g_ref, b_ref, o_ref, xb_ref):
    j = pl.program_id(1)

    @pl.when(j == 0)
    def _():
        xb_ref[...] = x_ref[...].astype(BF16)
        o_ref[...] = jnp.zeros_like(o_ref)

    h = jnp.maximum(_dot(xb_ref[...], w1_ref[...]), 0.0)
    o_ref[...] += _dot((h * h).astype(BF16), w2_ref[...])

    @pl.when(j == pl.num_programs(1) - 1)
    def _():
        y = DN_ALPHA * x_ref[...] + o_ref[...]
        o_ref[...] = _layer_norm(y, g_ref[...], b_ref[...])


def _mlp(x, w1, w2, g, b, tm=1024, tf=512):
    t, d = x.shape
    f = w1.shape[1]
    tm, tf = _tile(t, tm), _tile(f, tf)
    return pl.pallas_call(
        _mlp_kernel,
        grid=(t // tm, f // tf),
        in_specs=[pl.BlockSpec((tm, d), lambda i, j: (i, 0)),
                  pl.BlockSpec((d, tf), lambda i, j: (0, j)),
                  pl.BlockSpec((tf, d), lambda i, j: (j, 0)),
                  pl.BlockSpec((1, d), lambda i, j: (0, 0)),
                  pl.BlockSpec((1, d), lambda i, j: (0, 0))],
        out_specs=pl.BlockSpec((tm, d), lambda i, j: (i, 0)),
        out_shape=jax.ShapeDtypeStruct((t, d), F32),
        scratch_shapes=[pltpu.VMEM((tm, d), BF16)],
        compiler_params=_params("parallel", "arbitrary"),
        name="mlp",
    )(x, w1, w2, g, b)


def _qk_rope_kernel(x_ref, w_ref, gain_ref, cos_ref, sin_ref, o_ref, xb_ref):
    @pl.when(pl.program_id(1) == 0)
    def _():
        xb_ref[...] = x_ref[...].astype(BF16)

    pair = 2 * HEAD_DIM
    cos = jnp.tile(cos_ref[...], (1, 2))
    sin = jnp.tile(sin_ref[...], (1, 2))
    src = lax.broadcasted_iota(jnp.int32, (pair, pair), 0)
    dst = lax.broadcasted_iota(jnp.int32, (pair, pair), 1)
    quarter = HEAD_DIM // 4
    partner = jnp.where((dst % (2 * quarter)) < quarter, dst + quarter, dst - quarter)
    perm = jnp.where(src == partner, 1.0, 0.0).astype(BF16)
    y = _dot(xb_ref[...], w_ref[...])
    for c in range(o_ref.shape[1] // pair):
        heads = []
        for h in range(2):
            cols = slice(c * pair + h * HEAD_DIM, c * pair + (h + 1) * HEAD_DIM)
            t = y[:, cols]
            heads.append(t * lax.rsqrt(jnp.mean(t * t, axis=-1, keepdims=True) + EPS) * gain_ref[:, cols])
        t = jnp.concatenate(heads, axis=-1)
        t_hi = t.astype(BF16)
        t_lo = (t - t_hi.astype(F32)).astype(BF16)
        rot = _dot(t_hi, perm) + _dot(t_lo, perm)
        o_ref[:, c * pair:(c + 1) * pair] = (t * cos + rot * sin).astype(o_ref.dtype)


def _qk_rope(x, w, gains, cos, sin, seq, tm=512, tn=512):
    t, k = x.shape
    n = w.shape[1]
    tm, tn = _tile(seq, tm), _tile(n, tn)
    ns = seq // tm
    return pl.pallas_call(
        _qk_rope_kernel,
        grid=(t // tm, n // tn),
        in_specs=[pl.BlockSpec((tm, k), lambda i, j: (i, 0)),
                  pl.BlockSpec((k, tn), lambda i, j: (0, j)),
                  pl.BlockSpec((1, tn), lambda i, j: (0, j)),
                  pl.BlockSpec((tm, HEAD_DIM), lambda i, j: (i % ns, 0)),
                  pl.BlockSpec((tm, HEAD_DIM), lambda i, j: (i % ns, 0))],
        out_specs=pl.BlockSpec((tm, tn), lambda i, j: (i, j)),
        out_shape=jax.ShapeDtypeStruct((t, n), BF16),
        scratch_shapes=[pltpu.VMEM((tm, k), BF16)],
        compiler_params=_params("parallel", "arbitrary"),
        name="qk_rope",
    )(x, w, gains, cos, sin)


SAFE_LOG2 = 60.0


def _gqa_kernel(safe_ref, q_ref, k_ref, v_ref, o_ref, vx_ref, m_ref, acc_ref, *, kb):
    nkb = k_ref.shape[0] // kb

    @pl.when(pl.program_id(2) == 0)
    def _():
        vx_ref[:, :HEAD_DIM] = v_ref[...]
        vx_ref[:, HEAD_DIM:] = jnp.ones_like(v_ref)

    safe = safe_ref[0] != 0

    @pl.when(safe)
    def _():
        def block(j, first):
            ks = pl.ds(pl.multiple_of(j * kb, kb), kb)
            k_blk = k_ref[ks, :]
            vx_blk = vx_ref[ks, :]
            for g in range(GQA_GROUP):
                s = _dot_nt(q_ref[:, g * HEAD_DIM:(g + 1) * HEAD_DIM], k_blk)
                pv = _dot(jnp.exp2(s).astype(BF16), vx_blk)
                acc_ref[g] = pv if first else acc_ref[g] + pv

        block(0, True)

        def body(j, carry):
            block(j, False)
            return carry

        lax.fori_loop(1, nkb, body, 0)

    @pl.when(jnp.logical_not(safe))
    def _():
        m_ref[...] = jnp.full_like(m_ref, -jnp.inf)
        acc_ref[...] = jnp.zeros_like(acc_ref)

        def body(j, carry):
            ks = pl.ds(pl.multiple_of(j * kb, kb), kb)
            k_blk = k_ref[ks, :]
            vx_blk = vx_ref[ks, :]
            for g in range(GQA_GROUP):
                s = _dot_nt(q_ref[:, g * HEAD_DIM:(g + 1) * HEAD_DIM], k_blk)
                m_prev = m_ref[g]
                m_new = jnp.maximum(m_prev, jnp.max(s, axis=-1, keepdims=True))
                alpha = jnp.exp2(m_prev - m_new)
                p = jnp.exp2(s - jnp.tile(m_new, (1, kb // LANES)))
                acc_ref[g] = jnp.tile(alpha, (1, 2)) * acc_ref[g] + _dot(p.astype(BF16), vx_blk)
                m_ref[g] = m_new
            return carry

        lax.fori_loop(0, nkb, body, 0)

    for g in range(GQA_GROUP):
        acc = acc_ref[g]
        o_ref[:, g * HEAD_DIM:(g + 1) * HEAD_DIM] = (acc[:, :HEAD_DIM] / acc[:, HEAD_DIM:]).astype(o_ref.dtype)


def _gqa(safe, qk, v, batch, seq, d_model, tq=256, kb=2048):
    n_kv = d_model // HEAD_DIM // GQA_GROUP
    tq, kb = _tile(seq, tq), _tile(seq, kb)
    nq = seq // tq
    gw = GQA_GROUP * HEAD_DIM
    k0 = d_model // HEAD_DIM
    scratch = [pltpu.VMEM((seq, 2 * HEAD_DIM), BF16),
               pltpu.VMEM((GQA_GROUP, tq, HEAD_DIM), F32),
               pltpu.VMEM((GQA_GROUP, tq, 2 * HEAD_DIM), F32)]
    return pl.pallas_call(
        functools.partial(_gqa_kernel, kb=kb),
        grid=(batch, n_kv, nq),
        in_specs=[pl.BlockSpec(memory_space=pltpu.SMEM),
                  pl.BlockSpec((tq, gw), lambda b, h, i: (b * nq + i, h)),
                  pl.BlockSpec((seq, HEAD_DIM), lambda b, h, i: (b, k0 + h)),
                  pl.BlockSpec((seq, HEAD_DIM), lambda b, h, i: (b, h))],
        out_specs=pl.BlockSpec((tq, gw), lambda b, h, i: (b * nq + i, h)),
        out_shape=jax.ShapeDtypeStruct((batch * seq, d_model), BF16),
        scratch_shapes=scratch,
        compiler_params=_params("parallel", "parallel", "arbitrary"),
        name="gqa",
    )(safe, qk, qk, v)


def _gqa_safe_flag(q_gain_scaled, k_gain):
    bound = 1.02 * HEAD_DIM * jnp.max(jnp.abs(q_gain_scaled), axis=-1) * jnp.max(jnp.abs(k_gain), axis=-1)
    return (bound <= SAFE_LOG2).astype(jnp.int32)[:, None]


def _rope_tables(n):
    axis_dim = HEAD_DIM // 2
    rows = n // GRID_W
    row = jnp.repeat(jnp.arange(rows, dtype=F32), GRID_W)
    col = jnp.tile(jnp.arange(GRID_W, dtype=F32), rows)
    inv = ROPE_THETA ** (-jnp.arange(0, axis_dim, 2, dtype=F32) / axis_dim)
    ar = row[:, None] * inv
    ac = col[:, None] * inv
    ang = jnp.concatenate([ar, ar, ac, ac], axis=-1)
    lane = jnp.arange(HEAD_DIM)
    sign = jnp.where((lane % axis_dim) < axis_dim // 2, -1.0, 1.0).astype(F32)
    return jnp.cos(ang), jnp.sin(ang) * sign


def _prep_weights(gla_w_in, gla_w_gate_up, gla_b_gate, gla_norm_g, gla_w_out,
                  att_w_qkv, att_q_gain, att_k_gain, att_w_out,
                  mem_w_q, mem_w_kv, mem_w_o, mlp_w1, mlp_w2, ln_g, ln_b):
    d_model = gla_w_in.shape[1]
    n_main = 3 * d_model
    dk_all = d_model // 2
    p = {}
    p["gla_w_main"] = gla_w_in[:, :, :n_main].astype(BF16)
    wz = gla_w_in[:, :, n_main:]
    p["gla_w_z"] = jnp.pad(wz, ((0, 0), (0, 0), (0, LANES - wz.shape[-1]))).astype(BF16)
    n_gla = gla_w_gate_up.shape[0]
    wup = jnp.zeros((n_gla, 2, LANES, dk_all), F32)
    wup = wup.at[:, 0, :GATE_RANK].set(gla_w_gate_up[:, 0])
    wup = wup.at[:, 1, GATE_RANK:2 * GATE_RANK].set(gla_w_gate_up[:, 1])
    p["gla_wup"] = wup.astype(BF16)
    p["gla_bg"] = gla_b_gate.reshape(n_gla, 2, 1, dk_all)
    p["gla_ng"] = gla_norm_g.reshape(n_gla, 1, -1)
    p["gla_w_out"] = gla_w_out.astype(BF16)
    n_q = d_model // HEAD_DIM
    n_kv = n_q // GQA_GROUP
    n_qk = (n_q + n_kv) * HEAD_DIM
    p["att_w_qk"] = att_w_qkv[:, :, :n_qk].astype(BF16)
    p["att_w_v"] = att_w_qkv[:, :, n_qk:].astype(BF16)
    q_gain = att_q_gain * (HEAD_DIM ** -0.5 * math.log2(math.e))
    p["att_safe"] = _gqa_safe_flag(q_gain, att_k_gain)
    gains = jnp.concatenate([jnp.tile(q_gain, (1, n_q)), jnp.tile(att_k_gain, (1, n_kv))], axis=-1)
    p["att_gains"] = gains[:, None, :]
    p["att_w_out"] = att_w_out.astype(BF16)
    p["mem_w_q"] = mem_w_q.astype(BF16)
    p["mem_w_kv"] = mem_w_kv.astype(BF16)
    p["mem_w_o"] = mem_w_o.astype(BF16)
    p["mlp_w1"] = mlp_w1.astype(BF16)
    p["mlp_w2"] = mlp_w2.astype(BF16)
    p["ln_g"] = ln_g[:, :, None, :]
    p["ln_b"] = ln_b[:, :, None, :]
    return p


def _trunk(x3, mem3, p):
    batch, seq, d_model = x3.shape
    n_mem = mem3.shape[1]
    x = x3.reshape(batch * seq, d_model)
    mem = mem3.reshape(batch * n_mem, d_model)
    cos, sin = _rope_tables(seq)
    for i in range(DEPTH):
        j = i // 2
        if i % 2 == 0:
            proj, z = _gla_proj(x, p["gla_w_main"][j], p["gla_w_z"][j])
            o_f, o_b = _gla_scan(proj, z, p["gla_wup"][j], p["gla_bg"][j], batch, seq, d_model)
            x = _gla_out(o_f, o_b, proj, p["gla_ng"][j], p["gla_w_out"][j], x,
                         p["ln_g"][i, 0], p["ln_b"][i, 0])
        else:
            qk = _qk_rope(x, p["att_w_qk"][j], p["att_gains"][j], cos, sin, seq)
            v = _matmul(x, p["att_w_v"][j], BF16)
            h = _gqa(p["att_safe"][j], qk, v, batch, seq, d_model)
            x = _proj_ln(h, p["att_w_out"][j], x, p["ln_g"][i, 0], p["ln_b"][i, 0])
        kv = _matmul(mem, p["mem_w_kv"][i], BF16)
        x = _xattn(x, kv, p["mem_w_q"][i], p["mem_w_o"][i], p["ln_g"][i, 1], p["ln_b"][i, 1],
                   batch, seq, n_mem)
        x = _mlp(x, p["mlp_w1"][i], p["mlp_w2"][i], p["ln_g"][i, 2], p["ln_b"][i, 2])
    return x.reshape(batch, seq, d_model)


def kernel(x_prompt, x_sample, mem_prompt, mem_sample, gla_w_in, gla_w_gate_up, gla_b_gate, gla_norm_g, gla_w_out, att_w_qkv, att_q_gain, att_k_gain, att_w_out, mem_w_q, mem_w_kv, mem_w_o, mlp_w1, mlp_w2, ln_g, ln_b):
    p = _prep_weights(gla_w_in, gla_w_gate_up, gla_b_gate, gla_norm_g, gla_w_out,
                      att_w_qkv, att_q_gain, att_k_gain, att_w_out,
                      mem_w_q, mem_w_kv, mem_w_o, mlp_w1, mlp_w2, ln_g, ln_b)
    return (_trunk(x_prompt, mem_prompt, p), _trunk(x_sample, mem_sample, p))
```

```python
import functools
import math

import jax
import jax.numpy as jnp
from jax import lax
from jax.experimental import pallas as pl
from jax.experimental.pallas import tpu as pltpu

F32 = jnp.float32
BF16 = jnp.bfloat16

DEPTH = 2
GRID_W = 64
X_HEADS = 4
GLA_HEADS = 4
GATE_RANK = 16
GATE_NORM = 16.0
CHUNK = 64
HEAD_DIM = 128
GQA_GROUP = 4
ROPE_THETA = 10000.0
DN_ALPHA = (2.0 * DEPTH) ** 0.25
EPS = 1e-5

LANES = 128
VMEM_LIMIT = 56 * 1024 * 1024


def _params(*sem):
    return pltpu.CompilerParams(dimension_semantics=sem, vmem_limit_bytes=VMEM_LIMIT)


def _dot(a, b):
    return jnp.dot(a, b, preferred_element_type=F32)


def _dot_nt(a, b):
    return lax.dot_general(a, b, (((1,), (1,)), ((), ())), preferred_element_type=F32)


def _dot_tn(a, b):
    return lax.dot_general(a, b, (((0,), (0,)), ((), ())), preferred_element_type=F32)


def _layer_norm(y, g, b):
    mu = jnp.mean(y, axis=-1, keepdims=True)
    d = y - mu
    var = jnp.mean(d * d, axis=-1, keepdims=True)
    return d * lax.rsqrt(var + EPS) * g + b


def _tile(n, want):
    t = min(n, want)
    assert n % t == 0, (n, t)
    return t


def _matmul_kernel(x_ref, w_ref, o_ref, xb_ref):
    @pl.when(pl.program_id(1) == 0)
    def _():
        xb_ref[...] = x_ref[...].astype(BF16)

    o_ref[...] = _dot(xb_ref[...], w_ref[...]).astype(o_ref.dtype)


def _matmul(x, w, out_dtype, tm=1024, tn=1024):
    t, k = x.shape
    n = w.shape[1]
    tm, tn = _tile(t, tm), _tile(n, tn)
    return pl.pallas_call(
        _matmul_kernel,
        grid=(t // tm, n // tn),
        in_specs=[pl.BlockSpec((tm, k), lambda i, j: (i, 0)),
                  pl.BlockSpec((k, tn), lambda i, j: (0, j))],
        out_specs=pl.BlockSpec((tm, tn), lambda i, j: (i, j)),
        out_shape=jax.ShapeDtypeStruct((t, n), out_dtype),
        scratch_shapes=[pltpu.VMEM((tm, k), BF16)],
        compiler_params=_params("parallel", "arbitrary"),
        name="matmul",
    )(x, w)


def _gla_proj_kernel(x_ref, w_ref, wz_ref, o_ref, z_ref, xb_ref):
    @pl.when(pl.program_id(1) == 0)
    def _():
        xb = x_ref[...].astype(BF16)
        xb_ref[...] = xb
        z_ref[...] = _dot(xb, wz_ref[...])

    o_ref[...] = _dot(xb_ref[...], w_ref[...]).astype(o_ref.dtype)


def _gla_proj(x, w_main, w_z, tm=1024, tn=1024):
    t, k = x.shape
    n = w_main.shape[1]
    tm, tn = _tile(t, tm), _tile(n, tn)
    return pl.pallas_call(
        _gla_proj_kernel,
        grid=(t // tm, n // tn),
        in_specs=[pl.BlockSpec((tm, k), lambda i, j: (i, 0)),
                  pl.BlockSpec((k, tn), lambda i, j: (0, j)),
                  pl.BlockSpec((k, LANES), lambda i, j: (0, 0))],
        out_specs=[pl.BlockSpec((tm, tn), lambda i, j: (i, j)),
                   pl.BlockSpec((tm, LANES), lambda i, j: (i, 0))],
        out_shape=[jax.ShapeDtypeStruct((t, n), BF16),
                   jax.ShapeDtypeStruct((t, LANES), F32)],
        scratch_shapes=[pltpu.VMEM((tm, k), BF16)],
        compiler_params=_params("parallel", "arbitrary"),
        name="gla_proj",
    )(x, w_main, w_z)


GLA_HEADS_PER_STEP = 2
GLA_CHUNKS_PER_TILE = 4


def _log_sigmoid(x):
    return jnp.minimum(x, 0.0) - jnp.log1p(jnp.exp(-jnp.abs(x)))


def _gla_kernel(qf_ref, kf_ref, vf_ref, zf_ref, qb_ref, kb_ref, vb_ref, zb_ref, wup_ref, bg_ref,
                of_ref, ob_ref, st_ref, g_ref, qe_ref, kd_ref, dec_ref, *, nchunks, dk, dv):
    hp = GLA_HEADS_PER_STEP
    dirs = ((qf_ref, kf_ref, vf_ref, zf_ref, of_ref), (qb_ref, kb_ref, vb_ref, zb_ref, ob_ref))

    @pl.when(pl.program_id(2) == 0)
    def _():
        st_ref[...] = jnp.zeros_like(st_ref)

    for d in range(2):
        pre = _dot(dirs[d][3][...].astype(BF16), wup_ref[d]) + bg_ref[d]
        g_ref[d] = _log_sigmoid(pre) * (1.0 / GATE_NORM)

    tile = GLA_CHUNKS_PER_TILE * CHUNK
    ri = lax.broadcasted_iota(jnp.int32, (tile, tile), 0)
    ci = lax.broadcasted_iota(jnp.int32, (tile, tile), 1)
    same = (ri // CHUNK) == (ci // CHUNK)
    cum_masks = (same & (ci <= ri), same & (ci >= ri))
    att_masks = (same & (ci <= ri), same & (ci > ri))
    tris = [jnp.where(m, 1.0, 0.0).astype(BF16) for m in cum_masks]
    scale = dk ** -0.5

    for d in range(2):
        q_ref, k_ref, v_ref, _, o_ref = dirs[d]
        for hh in range(hp):
            kcols = slice(hh * dk, (hh + 1) * dk)
            vcols = slice(hh * dv, (hh + 1) * dv)
            for t in range(nchunks // GLA_CHUNKS_PER_TILE):
                rows = slice(t * tile, (t + 1) * tile)
                g = g_ref[d, rows, kcols]
                g_hi = g.astype(BF16)
                g_lo = (g - g_hi.astype(F32)).astype(BF16)
                bc = _dot(tris[d], g_hi) + _dot(tris[d], g_lo)
                tot = jnp.concatenate(
                    [jnp.broadcast_to(bc[r:r + 1, :], (CHUNK, dk))
                     for r in range(0 if d else CHUNK - 1, tile, CHUNK)], axis=0)
                q = q_ref[rows, kcols].astype(F32) * scale
                k = k_ref[rows, kcols].astype(F32)
                qe = (q * jnp.exp(bc)).astype(BF16)
                ke = (k * jnp.exp(-bc)).astype(BF16)
                a = jnp.where(att_masks[d], _dot_nt(qe, ke), 0.0).astype(BF16)
                o_ref[rows, vcols] = _dot(a, v_ref[rows, vcols])
                qe_ref[d * hp + hh, rows, :] = qe
                kd_ref[d * hp + hh, rows, :] = (k * jnp.exp(tot - bc)).astype(BF16)
                dec_ref[d * hp + hh, rows, :] = jnp.exp(tot)

    def chunk_step(d, hh, c):
        _, _, v_ref, _, o_ref = dirs[d]
        idx = d * hp + hh
        cc = (nchunks - 1 - c) if d else c
        rows = pl.ds(pl.multiple_of(cc * CHUNK, CHUNK), CHUNK)
        vcols = slice(hh * dv, (hh + 1) * dv)
        st = st_ref[idx]
        o_ref[rows, vcols] += _dot_nt(qe_ref[idx, rows, :], st.astype(BF16))
        dec = dec_ref[idx, pl.ds(pl.multiple_of(cc * CHUNK, CHUNK), 1), :]
        st_ref[idx] = st * dec + _dot_tn(v_ref[rows, vcols], kd_ref[idx, rows, :])

    def body(c, carry):
        for d in range(2):
            for hh in range(hp):
                chunk_step(d, hh, c)
        return carry

    lax.fori_loop(0, nchunks, body, 0, unroll=2)


def _gla_scan(proj, z, wup, bg, batch, seq, d_model, rows=512):
    dk = d_model // 2 // GLA_HEADS
    dv = d_model // GLA_HEADS
    hp = GLA_HEADS_PER_STEP
    rows = _tile(seq, rows)
    nb = seq // rows
    kk = d_model // 2 // (hp * dk)
    kv = d_model // (hp * dv)

    def fwd(b, i):
        return b * nb + i

    def bwd(b, i):
        return b * nb + nb - 1 - i

    def dir_specs(row):
        return [pl.BlockSpec((rows, hp * dk), lambda b, h, i: (row(b, i), h)),
                pl.BlockSpec((rows, hp * dk), lambda b, h, i: (row(b, i), kk + h)),
                pl.BlockSpec((rows, hp * dv), lambda b, h, i: (row(b, i), kv + h)),
                pl.BlockSpec((rows, LANES), lambda b, h, i: (row(b, i), 0))]

    out_sds = jax.ShapeDtypeStruct((batch * seq, d_model), F32)
    return pl.pallas_call(
        functools.partial(_gla_kernel, nchunks=rows // CHUNK, dk=dk, dv=dv),
        grid=(batch, GLA_HEADS // hp, nb),
        in_specs=dir_specs(fwd) + dir_specs(bwd) + [
            pl.BlockSpec((2, LANES, hp * dk), lambda b, h, i: (0, 0, h)),
            pl.BlockSpec((2, 1, hp * dk), lambda b, h, i: (0, 0, h))],
        out_specs=[pl.BlockSpec((rows, hp * dv), lambda b, h, i: (fwd(b, i), h)),
                   pl.BlockSpec((rows, hp * dv), lambda b, h, i: (bwd(b, i), h))],
        out_shape=[out_sds, out_sds],
        scratch_shapes=[pltpu.VMEM((2 * hp, dv, dk), F32),
                        pltpu.VMEM((2, rows, hp * dk), F32),
                        pltpu.VMEM((2 * hp, rows, dk), BF16),
                        pltpu.VMEM((2 * hp, rows, dk), BF16),
                        pltpu.VMEM((2 * hp, rows, dk), F32)],
        compiler_params=_params("parallel", "parallel", "arbitrary"),
        name="gla_scan",
    )(proj, proj, proj, z, proj, proj, proj, z, wup, bg)


def _gla_out_kernel(of_ref, ob_ref, r_ref, ng_ref, w_ref, x_ref, g_ref, b_ref, o_ref, a_ref):
    dv = ng_ref.shape[1]
    for h in range(GLA_HEADS):
        cols = slice(h * dv, (h + 1) * dv)
        o = of_ref[:, cols] + ob_ref[:, cols]
        o = o * lax.rsqrt(jnp.mean(o * o, axis=-1, keepdims=True) + EPS) * ng_ref[...]
        r = r_ref[:, cols].astype(F32)
        a_ref[:, cols] = (o * (r / (1.0 + jnp.exp(-r)))).astype(BF16)
    y = DN_ALPHA * x_ref[...] + _dot(a_ref[...], w_ref[...])
    o_ref[...] = _layer_norm(y, g_ref[...], b_ref[...])


def _gla_out(o_f, o_b, proj, ng, w, x, g, b, tm=512):
    t, d = x.shape
    tm = _tile(t, tm)
    r_block = proj.shape[1] // d - 1
    row = lambda i: (i, 0)
    const = lambda i: (0, 0)
    return pl.pallas_call(
        _gla_out_kernel,
        grid=(t // tm,),
        in_specs=[pl.BlockSpec((tm, d), row),
                  pl.BlockSpec((tm, d), row),
                  pl.BlockSpec((tm, d), lambda i: (i, r_block)),
                  pl.BlockSpec(ng.shape, const),
                  pl.BlockSpec(w.shape, const, pipeline_mode=pl.Buffered(1)),
                  pl.BlockSpec((tm, d), row),
                  pl.BlockSpec((1, d), const),
                  pl.BlockSpec((1, d), const)],
        out_specs=pl.BlockSpec((tm, d), row),
        out_shape=jax.ShapeDtypeStruct((t, d), F32),
        scratch_shapes=[pltpu.VMEM((tm, d), BF16)],
        compiler_params=_params("parallel"),
        name="gla_out",
    )(o_f, o_b, proj, ng, w, x, g, b)


def _proj_ln_kernel(a_ref, w_ref, x_ref, g_ref, b_ref, o_ref):
    y = DN_ALPHA * x_ref[...] + _dot(a_ref[...], w_ref[...])
    o_ref[...] = _layer_norm(y, g_ref[...], b_ref[...])


def _proj_ln(a, w, x, g, b, tm=512):
    t, k = a.shape
    d = w.shape[1]
    tm = _tile(t, tm)
    return pl.pallas_call(
        _proj_ln_kernel,
        grid=(t // tm,),
        in_specs=[pl.BlockSpec((tm, k), lambda i: (i, 0)),
                  pl.BlockSpec((k, d), lambda i: (0, 0)),
                  pl.BlockSpec((tm, d), lambda i: (i, 0)),
                  pl.BlockSpec((1, d), lambda i: (0, 0)),
                  pl.BlockSpec((1, d), lambda i: (0, 0))],
        out_specs=pl.BlockSpec((tm, d), lambda i: (i, 0)),
        out_shape=jax.ShapeDtypeStruct((t, d), F32),
        compiler_params=_params("parallel"),
        name="proj_ln",
    )(a, w, x, g, b)


def _xattn_kernel(x_ref, k_ref, v_ref, wq_ref, wo_ref, g_ref, b_ref, o_ref, a_ref, *, dh):
    scale = dh ** -0.5
    xb = x_ref[...].astype(BF16)
    for h in range(X_HEADS):
        cols = slice(h * dh, (h + 1) * dh)
        q = _dot(xb, wq_ref[:, cols]).astype(BF16)
        s = _dot_nt(q, k_ref[:, cols]) * scale
        p = jnp.exp(s - jnp.max(s, axis=-1, keepdims=True))
        l = jnp.sum(p, axis=-1, keepdims=True)
        a_ref[:, cols] = (_dot(p.astype(BF16), v_ref[:, cols]) / l).astype(BF16)
    y = DN_ALPHA * x_ref[...] + _dot(a_ref[...], wo_ref[...])
    o_ref[...] = _layer_norm(y, g_ref[...], b_ref[...])


def _xattn(x, kv, w_q, w_o, g, b, batch, seq, n_mem, tq=512):
    d_model = x.shape[1]
    tq = _tile(seq, tq)
    nq = seq // tq
    row = lambda bi, i: (bi * nq + i, 0)
    const = lambda bi, i: (0, 0)
    return pl.pallas_call(
        functools.partial(_xattn_kernel, dh=d_model // X_HEADS),
        grid=(batch, nq),
        in_specs=[pl.BlockSpec((tq, d_model), row),
                  pl.BlockSpec((n_mem, d_model), lambda bi, i: (bi, 0)),
                  pl.BlockSpec((n_mem, d_model), lambda bi, i: (bi, 1)),
                  pl.BlockSpec(w_q.shape, const, pipeline_mode=pl.Buffered(1)),
                  pl.BlockSpec(w_o.shape, const, pipeline_mode=pl.Buffered(1)),
                  pl.BlockSpec((1, d_model), const),
                  pl.BlockSpec((1, d_model), const)],
        out_specs=pl.BlockSpec((tq, d_model), row),
        out_shape=jax.ShapeDtypeStruct((batch * seq, d_model), F32),
        scratch_shapes=[pltpu.VMEM((tq, d_model), BF16)],
        compiler_params=_params("parallel", "parallel"),
        name="xattn",
    )(x, kv, kv, w_q, w_o, g, b)


def _mlp_kernel(x_ref, w1_ref, w2_ref, g_ref, b_ref, o_ref, xb_ref):
    j = pl.program_id(1)

    @pl.when(j == 0)
    def _():
        xb_ref[...] = x_ref[...].astype(BF16)
        o_ref[...] = jnp.zeros_like(o_ref)

    h = jnp.maximum(_dot(xb_ref[...], w1_ref[...]), 0.0)
    o_ref[...] += _dot((h * h).astype(BF16), w2_ref[...])

    @pl.when(j == pl.num_programs(1) - 1)
    def _():
        y = DN_ALPHA * x_ref[...] + o_ref[...]
        o_ref[...] = _layer_norm(y, g_ref[...], b_ref[...])


def _mlp(x, w1, w2, g, b, tm=1024, tf=512):
    t, d = x.shape
    f = w1.shape[1]
    tm, tf = _tile(t, tm), _tile(f, tf)
    return pl.pallas_call(
        _mlp_kernel,
        grid=(t // tm, f // tf),
        in_specs=[pl.BlockSpec((tm, d), lambda i, j: (i, 0)),
                  pl.BlockSpec((d, tf), lambda i, j: (0, j)),
                  pl.BlockSpec((tf, d), lambda i, j: (j, 0)),
                  pl.BlockSpec((1, d), lambda i, j: (0, 0)),
                  pl.BlockSpec((1, d), lambda i, j: (0, 0))],
        out_specs=pl.BlockSpec((tm, d), lambda i, j: (i, 0)),
        out_shape=jax.ShapeDtypeStruct((t, d), F32),
        scratch_shapes=[pltpu.VMEM((tm, d), BF16)],
        compiler_params=_params("parallel", "arbitrary"),
        name="mlp",
    )(x, w1, w2, g, b)


def _qk_rope_kernel(x_ref, w_ref, gain_ref, cos_ref, sin_ref, o_ref, xb_ref):
    @pl.when(pl.program_id(1) == 0)
    def _():
        xb_ref[...] = x_ref[...].astype(BF16)

    pair = 2 * HEAD_DIM
    cos = jnp.tile(cos_ref[...], (1, 2))
    sin = jnp.tile(sin_ref[...], (1, 2))
    src = lax.broadcasted_iota(jnp.int32, (pair, pair), 0)
    dst = lax.broadcasted_iota(jnp.int32, (pair, pair), 1)
    quarter = HEAD_DIM // 4
    partner = jnp.where((dst % (2 * quarter)) < quarter, dst + quarter, dst - quarter)
    perm = jnp.where(src == partner, 1.0, 0.0).astype(BF16)
    y = _dot(xb_ref[...], w_ref[...])
    for c in range(o_ref.shape[1] // pair):
        heads = []
        for h in range(2):
            cols = slice(c * pair + h * HEAD_DIM, c * pair + (h + 1) * HEAD_DIM)
            t = y[:, cols]
            heads.append(t * lax.rsqrt(jnp.mean(t * t, axis=-1, keepdims=True) + EPS) * gain_ref[:, cols])
        t = jnp.concatenate(heads, axis=-1)
        t_hi = t.astype(BF16)
        t_lo = (t - t_hi.astype(F32)).astype(BF16)
        rot = _dot(t_hi, perm) + _dot(t_lo, perm)
        o_ref[:, c * pair:(c + 1) * pair] = (t * cos + rot * sin).astype(o_ref.dtype)


def _qk_rope(x, w, gains, cos, sin, seq, tm=512, tn=512):
    t, k = x.shape
    n = w.shape[1]
    tm, tn = _tile(seq, tm), _tile(n, tn)
    ns = seq // tm
    return pl.pallas_call(
        _qk_rope_kernel,
        grid=(t // tm, n // tn),
        in_specs=[pl.BlockSpec((tm, k), lambda i, j: (i, 0)),
                  pl.BlockSpec((k, tn), lambda i, j: (0, j)),
                  pl.BlockSpec((1, tn), lambda i, j: (0, j)),
                  pl.BlockSpec((tm, HEAD_DIM), lambda i, j: (i % ns, 0)),
                  pl.BlockSpec((tm, HEAD_DIM), lambda i, j: (i % ns, 0))],
        out_specs=pl.BlockSpec((tm, tn), lambda i, j: (i, j)),
        out_shape=jax.ShapeDtypeStruct((t, n), BF16),
        scratch_shapes=[pltpu.VMEM((tm, k), BF16)],
        compiler_params=_params("parallel", "arbitrary"),
        name="qk_rope",
    )(x, w, gains, cos, sin)


SAFE_LOG2 = 60.0


def _gqa_kernel(safe_ref, q_ref, k_ref, v_ref, o_ref, vx_ref, m_ref, acc_ref, *, kb):
    nkb = k_ref.shape[0] // kb

    @pl.when(pl.program_id(2) == 0)
    def _():
        vx_ref[:, :HEAD_DIM] = v_ref[...]
        vx_ref[:, HEAD_DIM:] = jnp.ones_like(v_ref)

    safe = safe_ref[0] != 0

    @pl.when(safe)
    def _():
        def block(j, first):
            ks = pl.ds(pl.multiple_of(j * kb, kb), kb)
            k_blk = k_ref[ks, :]
            vx_blk = vx_ref[ks, :]
            for g in range(GQA_GROUP):
                s = _dot_nt(q_ref[:, g * HEAD_DIM:(g + 1) * HEAD_DIM], k_blk)
                pv = _dot(jnp.exp2(s).astype(BF16), vx_blk)
                acc_ref[g] = pv if first else acc_ref[g] + pv

        block(0, True)

        def body(j, carry):
            block(j, False)
            return carry

        lax.fori_loop(1, nkb, body, 0)

    @pl.when(jnp.logical_not(safe))
    def _():
        m_ref[...] = jnp.full_like(m_ref, -jnp.inf)
        acc_ref[...] = jnp.zeros_like(acc_ref)

        def body(j, carry):
            ks = pl.ds(pl.multiple_of(j * kb, kb), kb)
            k_blk = k_ref[ks, :]
            vx_blk = vx_ref[ks, :]
            for g in range(GQA_GROUP):
                s = _dot_nt(q_ref[:, g * HEAD_DIM:(g + 1) * HEAD_DIM], k_blk)
                m_prev = m_ref[g]
                m_new = jnp.maximum(m_prev, jnp.max(s, axis=-1, keepdims=True))
                alpha = jnp.exp2(m_prev - m_new)
                p = jnp.exp2(s - jnp.tile(m_new, (1, kb // LANES)))
                acc_ref[g] = jnp.tile(alpha, (1, 2)) * acc_ref[g] + _dot(p.astype(BF16), vx_blk)
                m_ref[g] = m_new
            return carry

        lax.fori_loop(0, nkb, body, 0)

    for g in range(GQA_GROUP):
        acc = acc_ref[g]
        o_ref[:, g * HEAD_DIM:(g + 1) * HEAD_DIM] = (acc[:, :HEAD_DIM] / acc[:, HEAD_DIM:]).astype(o_ref.dtype)


def _gqa(safe, qk, v, batch, seq, d_model, tq=512, kb=2048):
    n_kv = d_model // HEAD_DIM // GQA_GROUP
    tq, kb = _tile(seq, tq), _tile(seq, kb)
    nq = seq // tq
    gw = GQA_GROUP * HEAD_DIM
    k0 = d_model // HEAD_DIM
    scratch = [pltpu.VMEM((seq, 2 * HEAD_DIM), BF16),
               pltpu.VMEM((GQA_GROUP, tq, HEAD_DIM), F32),
               pltpu.VMEM((GQA_GROUP, tq, 2 * HEAD_DIM), F32)]
    return pl.pallas_call(
        functools.partial(_gqa_kernel, kb=kb),
        grid=(batch, n_kv, nq),
        in_specs=[pl.BlockSpec(memory_space=pltpu.SMEM),
                  pl.BlockSpec((tq, gw), lambda b, h, i: (b * nq + i, h)),
                  pl.BlockSpec((seq, HEAD_DIM), lambda b, h, i: (b, k0 + h)),
                  pl.BlockSpec((seq, HEAD_DIM), lambda b, h, i: (b, h))],
        out_specs=pl.BlockSpec((tq, gw), lambda b, h, i: (b * nq + i, h)),
        out_shape=jax.ShapeDtypeStruct((batch * seq, d_model), BF16),
        scratch_shapes=scratch,
        compiler_params=_params("parallel", "parallel", "arbitrary"),
        name="gqa",
    )(safe, qk, qk, v)


def _gqa_safe_flag(q_gain_scaled, k_gain):
    bound = 1.02 * HEAD_DIM * jnp.max(jnp.abs(q_gain_scaled), axis=-1) * jnp.max(jnp.abs(k_gain), axis=-1)
    return (bound <= SAFE_LOG2).astype(jnp.int32)[:, None]


def _rope_tables(n):
    axis_dim = HEAD_DIM // 2
    rows = n // GRID_W
    row = jnp.repeat(jnp.arange(rows, dtype=F32), GRID_W)
    col = jnp.tile(jnp.arange(GRID_W, dtype=F32), rows)
    inv = ROPE_THETA ** (-jnp.arange(0, axis_dim, 2, dtype=F32) / axis_dim)
    ar = row[:, None] * inv
    ac = col[:, None] * inv
    ang = jnp.concatenate([ar, ar, ac, ac], axis=-1)
    lane = jnp.arange(HEAD_DIM)
    sign = jnp.where((lane % axis_dim) < axis_dim // 2, -1.0, 1.0).astype(F32)
    return jnp.cos(ang), jnp.sin(ang) * sign


def _prep_weights(gla_w_in, gla_w_gate_up, gla_b_gate, gla_norm_g, gla_w_out,
                  att_w_qkv, att_q_gain, att_k_gain, att_w_out,
                  mem_w_q, mem_w_kv, mem_w_o, mlp_w1, mlp_w2, ln_g, ln_b):
    d_model = gla_w_in.shape[1]
    n_main = 3 * d_model
    dk_all = d_model // 2
    p = {}
    p["gla_w_main"] = gla_w_in[:, :, :n_main].astype(BF16)
    wz = gla_w_in[:, :, n_main:]
    p["gla_w_z"] = jnp.pad(wz, ((0, 0), (0, 0), (0, LANES - wz.shape[-1]))).astype(BF16)
    n_gla = gla_w_gate_up.shape[0]
    wup = jnp.zeros((n_gla, 2, LANES, dk_all), F32)
    wup = wup.at[:, 0, :GATE_RANK].set(gla_w_gate_up[:, 0])
    wup = wup.at[:, 1, GATE_RANK:2 * GATE_RANK].set(gla_w_gate_up[:, 1])
    p["gla_wup"] = wup.astype(BF16)
    p["gla_bg"] = gla_b_gate.reshape(n_gla, 2, 1, dk_all)
    p["gla_ng"] = gla_norm_g.reshape(n_gla, 1, -1)
    p["gla_w_out"] = gla_w_out.astype(BF16)
    n_q = d_model // HEAD_DIM
    n_kv = n_q // GQA_GROUP
    n_qk = (n_q + n_kv) * HEAD_DIM
    p["att_w_qk"] = att_w_qkv[:, :, :n_qk].astype(BF16)
    p["att_w_v"] = att_w_qkv[:, :, n_qk:].astype(BF16)
    q_gain = att_q_gain * (HEAD_DIM ** -0.5 * math.log2(math.e))
    p["att_safe"] = _gqa_safe_flag(q_gain, att_k_gain)
    gains = jnp.concatenate([jnp.tile(q_gain, (1, n_q)), jnp.tile(att_k_gain, (1, n_kv))], axis=-1)
    p["att_gains"] = gains[:, None, :]
    p["att_w_out"] = att_w_out.astype(BF16)
    p["mem_w_q"] = mem_w_q.astype(BF16)
    p["mem_w_kv"] = mem_w_kv.astype(BF16)
    p["mem_w_o"] = mem_w_o.astype(BF16)
    p["mlp_w1"] = mlp_w1.astype(BF16)
    p["mlp_w2"] = mlp_w2.astype(BF16)
    p["ln_g"] = ln_g[:, :, None, :]
    p["ln_b"] = ln_b[:, :, None, :]
    return p


def _trunk(x3, mem3, p):
    batch, seq, d_model = x3.shape
    n_mem = mem3.shape[1]
    x = x3.reshape(batch * seq, d_model)
    mem = mem3.reshape(batch * n_mem, d_model)
    cos, sin = _rope_tables(seq)
    for i in range(DEPTH):
        j = i // 2
        if i % 2 == 0:
            proj, z = _gla_proj(x, p["gla_w_main"][j], p["gla_w_z"][j])
            o_f, o_b = _gla_scan(proj, z, p["gla_wup"][j], p["gla_bg"][j], batch, seq, d_model)
            x = _gla_out(o_f, o_b, proj, p["gla_ng"][j], p["gla_w_out"][j], x,
                         p["ln_g"][i, 0], p["ln_b"][i, 0])
        else:
            qk = _qk_rope(x, p["att_w_qk"][j], p["att_gains"][j], cos, sin, seq)
            v = _matmul(x, p["att_w_v"][j], BF16)
            h = _gqa(p["att_safe"][j], qk, v, batch, seq, d_model)
            x = _proj_ln(h, p["att_w_out"][j], x, p["ln_g"][i, 0], p["ln_b"][i, 0])
        kv = _matmul(mem, p["mem_w_kv"][i], BF16)
        x = _xattn(x, kv, p["mem_w_q"][i], p["mem_w_o"][i], p["ln_g"][i, 1], p["ln_b"][i, 1],
                   batch, seq, n_mem)
        x = _mlp(x, p["mlp_w1"][i], p["mlp_w2"][i], p["ln_g"][i, 2], p["ln_b"][i, 2])
    return x.reshape(batch, seq, d_model)


def kernel(x_prompt, x_sample, mem_prompt, mem_sample, gla_w_in, gla_w_gate_up, gla_b_gate, gla_norm_g, gla_w_out, att_w_qkv, att_q_gain, att_k_gain, att_w_out, mem_w_q, mem_w_kv, mem_w_o, mlp_w1, mlp_w2, ln_g, ln_b):
    p = _prep_weights(gla_w_in, gla_w_gate_up, gla_b_gate, gla_norm_g, gla_w_out,
                      att_w_qkv, att_q_gain, att_k_gain, att_w_out,
                      mem_w_q, mem_w_kv, mem_w_o, mlp_w1, mlp_w2, ln_g, ln_b)
    return (_trunk(x_prompt, mem_prompt, p), _trunk(x_sample, mem_sample, p))
```

```python
import functools
import math

import jax
import jax.numpy as jnp
from jax import lax
from jax.experimental import pallas as pl
from jax.experimental.pallas import tpu as pltpu

F32 = jnp.float32
BF16 = jnp.bfloat16

DEPTH = 2
GRID_W = 64
X_HEADS = 4
GLA_HEADS = 4
GATE_RANK = 16
GATE_NORM = 16.0
CHUNK = 64
HEAD_DIM = 128
GQA_GROUP = 4
ROPE_THETA = 10000.0
DN_ALPHA = (2.0 * DEPTH) ** 0.25
EPS = 1e-5

LANES = 128
VMEM_LIMIT = 56 * 1024 * 1024


def _params(*sem):
    return pltpu.CompilerParams(dimension_semantics=sem, vmem_limit_bytes=VMEM_LIMIT)


def _dot(a, b):
    return jnp.dot(a, b, preferred_element_type=F32)


def _dot_nt(a, b):
    return lax.dot_general(a, b, (((1,), (1,)), ((), ())), preferred_element_type=F32)


def _dot_tn(a, b):
    return lax.dot_general(a, b, (((0,), (0,)), ((), ())), preferred_element_type=F32)


def _layer_norm(y, g, b):
    mu = jnp.mean(y, axis=-1, keepdims=True)
    d = y - mu
    var = jnp.mean(d * d, axis=-1, keepdims=True)
    return d * lax.rsqrt(var + EPS) * g + b


def _tile(n, want):
    t = min(n, want)
    assert n % t == 0, (n, t)
    return t


def _matmul_kernel(x_ref, w_ref, o_ref, xb_ref):
    @pl.when(pl.program_id(1) == 0)
    def _():
        xb_ref[...] = x_ref[...].astype(BF16)

    o_ref[...] = _dot(xb_ref[...], w_ref[...]).astype(o_ref.dtype)


def _matmul(x, w, out_dtype, tm=1024, tn=1024):
    t, k = x.shape
    n = w.shape[1]
    tm, tn = _tile(t, tm), _tile(n, tn)
    return pl.pallas_call(
        _matmul_kernel,
        grid=(t // tm, n // tn),
        in_specs=[pl.BlockSpec((tm, k), lambda i, j: (i, 0)),
                  pl.BlockSpec((k, tn), lambda i, j: (0, j))],
        out_specs=pl.BlockSpec((tm, tn), lambda i, j: (i, j)),
        out_shape=jax.ShapeDtypeStruct((t, n), out_dtype),
        scratch_shapes=[pltpu.VMEM((tm, k), BF16)],
        compiler_params=_params("parallel", "arbitrary"),
        name="matmul",
    )(x, w)


def _gla_proj_kernel(x_ref, w_ref, wz_ref, o_ref, z_ref, xb_ref):
    @pl.when(pl.program_id(1) == 0)
    def _():
        xb = x_ref[...].astype(BF16)
        xb_ref[...] = xb
        z_ref[...] = _dot(xb, wz_ref[...])

    o_ref[...] = _dot(xb_ref[...], w_ref[...]).astype(o_ref.dtype)


def _gla_proj(x, w_main, w_z, tm=1024, tn=1024):
    t, k = x.shape
    n = w_main.shape[1]
    tm, tn = _tile(t, tm), _tile(n, tn)
    return pl.pallas_call(
        _gla_proj_kernel,
        grid=(t // tm, n // tn),
        in_specs=[pl.BlockSpec((tm, k), lambda i, j: (i, 0)),
                  pl.BlockSpec((k, tn), lambda i, j: (0, j)),
                  pl.BlockSpec((k, LANES), lambda i, j: (0, 0))],
        out_specs=[pl.BlockSpec((tm, tn), lambda i, j: (i, j)),
                   pl.BlockSpec((tm, LANES), lambda i, j: (i, 0))],
        out_shape=[jax.ShapeDtypeStruct((t, n), BF16),
                   jax.ShapeDtypeStruct((t, LANES), F32)],
        scratch_shapes=[pltpu.VMEM((tm, k), BF16)],
        compiler_params=_params("parallel", "arbitrary"),
        name="gla_proj",
    )(x, w_main, w_z)


GLA_HEADS_PER_STEP = 2
GLA_CHUNKS_PER_TILE = 4


def _log_sigmoid(x):
    return jnp.minimum(x, 0.0) - jnp.log(1.0 + jnp.exp(-jnp.abs(x)))


def _gla_kernel(qf_ref, kf_ref, vf_ref, zf_ref, qb_ref, kb_ref, vb_ref, zb_ref, wup_ref, bg_ref,
                of_ref, ob_ref, st_ref, g_ref, qe_ref, kd_ref, dec_ref, *, nchunks, dk, dv):
    hp = GLA_HEADS_PER_STEP
    dirs = ((qf_ref, kf_ref, vf_ref, zf_ref, of_ref), (qb_ref, kb_ref, vb_ref, zb_ref, ob_ref))

    @pl.when(pl.program_id(2) == 0)
    def _():
        st_ref[...] = jnp.zeros_like(st_ref)

    for d in range(2):
        pre = _dot(dirs[d][3][...].astype(BF16), wup_ref[d]) + bg_ref[d]
        g_ref[d] = _log_sigmoid(pre) * (1.0 / GATE_NORM)

    tile = GLA_CHUNKS_PER_TILE * CHUNK
    ri = lax.broadcasted_iota(jnp.int32, (tile, tile), 0)
    ci = lax.broadcasted_iota(jnp.int32, (tile, tile), 1)
    same = (ri // CHUNK) == (ci // CHUNK)
    cum_masks = (same & (ci <= ri), same & (ci >= ri))
    att_masks = (same & (ci <= ri), same & (ci > ri))
    tris = [jnp.where(m, 1.0, 0.0).astype(BF16) for m in cum_masks]
    scale = dk ** -0.5

    for d in range(2):
        q_ref, k_ref, v_ref, _, o_ref = dirs[d]
        for hh in range(hp):
            kcols = slice(hh * dk, (hh + 1) * dk)
            vcols = slice(hh * dv, (hh + 1) * dv)
            for t in range(nchunks // GLA_CHUNKS_PER_TILE):
                rows = slice(t * tile, (t + 1) * tile)
                g = g_ref[d, rows, kcols]
                g_hi = g.astype(BF16)
                g_lo = (g - g_hi.astype(F32)).astype(BF16)
                bc = _dot(tris[d], g_hi) + _dot(tris[d], g_lo)
                tot = jnp.concatenate(
                    [jnp.broadcast_to(bc[r:r + 1, :], (CHUNK, dk))
                     for r in range(0 if d else CHUNK - 1, tile, CHUNK)], axis=0)
                q = q_ref[rows, kcols].astype(F32) * scale
                k = k_ref[rows, kcols].astype(F32)
                qe = (q * jnp.exp(bc)).astype(BF16)
                ke = (k * jnp.exp(-bc)).astype(BF16)
                a = jnp.where(att_masks[d], _dot_nt(qe, ke), 0.0).astype(BF16)
                o_ref[rows, vcols] = _dot(a, v_ref[rows, vcols])
                qe_ref[d * hp + hh, rows, :] = qe
                kd_ref[d * hp + hh, rows, :] = (k * jnp.exp(tot - bc)).astype(BF16)
                dec_ref[d * hp + hh, rows, :] = jnp.exp(tot)

    def chunk_step(d, hh, c):
        _, _, v_ref, _, o_ref = dirs[d]
        idx = d * hp + hh
        cc = (nchunks - 1 - c) if d else c
        rows = pl.ds(pl.multiple_of(cc * CHUNK, CHUNK), CHUNK)
        vcols = slice(hh * dv, (hh + 1) * dv)
        st = st_ref[idx]
        o_ref[rows, vcols] += _dot_nt(qe_ref[idx, rows, :], st.astype(BF16))
        dec = dec_ref[idx, pl.ds(pl.multiple_of(cc * CHUNK, CHUNK), 1), :]
        st_ref[idx] = st * dec + _dot_tn(v_ref[rows, vcols], kd_ref[idx, rows, :])

    def body(c, carry):
        for d in range(2):
            for hh in range(hp):
                chunk_step(d, hh, c)
        return carry

    lax.fori_loop(0, nchunks, body, 0, unroll=2)


def _gla_scan(proj, z, wup, bg, batch, seq, d_model, rows=512):
    dk = d_model // 2 // GLA_HEADS
    dv = d_model // GLA_HEADS
    hp = GLA_HEADS_PER_STEP
    rows = _tile(seq, rows)
    nb = seq // rows
    kk = d_model // 2 // (hp * dk)
    kv = d_model // (hp * dv)

    def fwd(b, i):
        return b * nb + i

    def bwd(b, i):
        return b * nb + nb - 1 - i

    def dir_specs(row):
        return [pl.BlockSpec((rows, hp * dk), lambda b, h, i: (row(b, i), h)),
                pl.BlockSpec((rows, hp * dk), lambda b, h, i: (row(b, i), kk + h)),
                pl.BlockSpec((rows, hp * dv), lambda b, h, i: (row(b, i), kv + h)),
                pl.BlockSpec((rows, LANES), lambda b, h, i: (row(b, i), 0))]

    out_sds = jax.ShapeDtypeStruct((batch * seq, d_model), F32)
    return pl.pallas_call(
        functools.partial(_gla_kernel, nchunks=rows // CHUNK, dk=dk, dv=dv),
        grid=(batch, GLA_HEADS // hp, nb),
        in_specs=dir_specs(fwd) + dir_specs(bwd) + [
            pl.BlockSpec((2, LANES, hp * dk), lambda b, h, i: (0, 0, h)),
            pl.BlockSpec((2, 1, hp * dk), lambda b, h, i: (0, 0, h))],
        out_specs=[pl.BlockSpec((rows, hp * dv), lambda b, h, i: (fwd(b, i), h)),
                   pl.BlockSpec((rows, hp * dv), lambda b, h, i: (bwd(b, i), h))],
        out_shape=[out_sds, out_sds],
        scratch_shapes=[pltpu.VMEM((2 * hp, dv, dk), F32),
                        pltpu.VMEM((2, rows, hp * dk), F32),
                        pltpu.VMEM((2 * hp, rows, dk), BF16),
                        pltpu.VMEM((2 * hp, rows, dk), BF16),
                        pltpu.VMEM((2 * hp, rows, dk), F32)],
        compiler_params=_params("parallel", "parallel", "arbitrary"),
        name="gla_scan",
    )(proj, proj, proj, z, proj, proj, proj, z, wup, bg)


def _gla_out_kernel(of_ref, ob_ref, r_ref, ng_ref, w_ref, x_ref, g_ref, b_ref, o_ref, a_ref):
    dv = ng_ref.shape[1]
    for h in range(GLA_HEADS):
        cols = slice(h * dv, (h + 1) * dv)
        o = of_ref[:, cols] + ob_ref[:, cols]
        o = o * lax.rsqrt(jnp.mean(o * o, axis=-1, keepdims=True) + EPS) * ng_ref[...]
        r = r_ref[:, cols].astype(F32)
        a_ref[:, cols] = (o * (r / (1.0 + jnp.exp(-r)))).astype(BF16)
    y = DN_ALPHA * x_ref[...] + _dot(a_ref[...], w_ref[...])
    o_ref[...] = _layer_norm(y, g_ref[...], b_ref[...])


def _gla_out(o_f, o_b, proj, ng, w, x, g, b, tm=512):
    t, d = x.shape
    tm = _tile(t, tm)
    r_block = proj.shape[1] // d - 1
    row = lambda i: (i, 0)
    const = lambda i: (0, 0)
    return pl.pallas_call(
        _gla_out_kernel,
        grid=(t // tm,),
        in_specs=[pl.BlockSpec((tm, d), row),
                  pl.BlockSpec((tm, d), row),
                  pl.BlockSpec((tm, d), lambda i: (i, r_block)),
                  pl.BlockSpec(ng.shape, const),
                  pl.BlockSpec(w.shape, const, pipeline_mode=pl.Buffered(1)),
                  pl.BlockSpec((tm, d), row),
                  pl.BlockSpec((1, d), const),
                  pl.BlockSpec((1, d), const)],
        out_specs=pl.BlockSpec((tm, d), row),
        out_shape=jax.ShapeDtypeStruct((t, d), F32),
        scratch_shapes=[pltpu.VMEM((tm, d), BF16)],
        compiler_params=_params("parallel"),
        name="gla_out",
    )(o_f, o_b, proj, ng, w, x, g, b)


def _proj_ln_kernel(a_ref, w_ref, x_ref, g_ref, b_ref, o_ref):
    y = DN_ALPHA * x_ref[...] + _dot(a_ref[...], w_ref[...])
    o_ref[...] = _layer_norm(y, g_ref[...], b_ref[...])


def _proj_ln(a, w, x, g, b, tm=512):
    t, k = a.shape
    d = w.shape[1]
    tm = _tile(t, tm)
    return pl.pallas_call(
        _proj_ln_kernel,
        grid=(t // tm,),
        in_specs=[pl.BlockSpec((tm, k), lambda i: (i, 0)),
                  pl.BlockSpec((k, d), lambda i: (0, 0)),
                  pl.BlockSpec((tm, d), lambda i: (i, 0)),
                  pl.BlockSpec((1, d), lambda i: (0, 0)),
                  pl.BlockSpec((1, d), lambda i: (0, 0))],
        out_specs=pl.BlockSpec((tm, d), lambda i: (i, 0)),
        out_shape=jax.ShapeDtypeStruct((t, d), F32),
        compiler_params=_params("parallel"),
        name="proj_ln",
    )(a, w, x, g, b)


def _xattn_kernel(x_ref, k_ref, v_ref, wq_ref, wo_ref, g_ref, b_ref, o_ref, a_ref, *, dh):
    scale = dh ** -0.5
    xb = x_ref[...].astype(BF16)
    for h in range(X_HEADS):
        cols = slice(h * dh, (h + 1) * dh)
        q = _dot(xb, wq_ref[:, cols]).astype(BF16)
        s = _dot_nt(q, k_ref[:, cols]) * scale
        p = jnp.exp(s - jnp.max(s, axis=-1, keepdims=True))
        l = jnp.sum(p, axis=-1, keepdims=True)
        a_ref[:, cols] = (_dot(p.astype(BF16), v_ref[:, cols]) / l).astype(BF16)
    y = DN_ALPHA * x_ref[...] + _dot(a_ref[...], wo_ref[...])
    o_ref[...] = _layer_norm(y, g_ref[...], b_ref[...])


def _xattn(x, kv, w_q, w_o, g, b, batch, seq, n_mem, tq=512):
    d_model = x.shape[1]
    tq = _tile(seq, tq)
    nq = seq // tq
    row = lambda bi, i: (bi * nq + i, 0)
    const = lambda bi, i: (0, 0)
    return pl.pallas_call(
        functools.partial(_xattn_kernel, dh=d_model // X_HEADS),
        grid=(batch, nq),
        in_specs=[pl.BlockSpec((tq, d_model), row),
                  pl.BlockSpec((n_mem, d_model), lambda bi, i: (bi, 0)),
                  pl.BlockSpec((n_mem, d_model), lambda bi, i: (bi, 1)),
                  pl.BlockSpec(w_q.shape, const, pipeline_mode=pl.Buffered(1)),
                  pl.BlockSpec(w_o.shape, const, pipeline_mode=pl.Buffered(1)),
                  pl.BlockSpec((1, d_model), const),
                  pl.BlockSpec((1, d_model), const)],
        out_specs=pl.BlockSpec((tq, d_model), row),
        out_shape=jax.ShapeDtypeStruct((batch * seq, d_model), F32),
        scratch_shapes=[pltpu.VMEM((tq, d_model), BF16)],
        compiler_params=_params("parallel", "parallel"),
        name="xattn",
    )(x, kv, kv, w_q, w_o, g, b)


def _mlp_kernel(x_ref, w1_ref, w2_ref, g_ref, b_ref, o_ref, xb_ref):
    j = pl.program_id(1)

    @pl.when(j == 0)
    def _():
        xb_ref[...] = x_ref[...].astype(BF16)
        o_ref[...] = jnp.zeros_like(o_ref)

    h = jnp.maximum(_dot(xb_ref[...], w1_ref[...]), 0.0)
    o_ref[...] += _dot((h * h).astype(BF16), w2_ref[...])

    @pl.when(j == pl.num_programs(1) - 1)
    def _():
        y = DN_ALPHA * x_ref[...] + o_ref[...]
        o_ref[...] = _layer_norm(y, g_ref[...], b_ref[...])


def _mlp(x, w1, w2, g, b, tm=1024, tf=512):
    t, d = x.shape
    f = w1.shape[1]
    tm, tf = _tile(t, tm), _tile(f, tf)
    return pl.pallas_call(
        _mlp_kernel,
        grid=(t // tm, f // tf),
        in_specs=[pl.BlockSpec((tm, d), lambda i, j: (i, 0)),
                  pl.BlockSpec((d, tf), lambda i, j: (0, j)),
                  pl.BlockSpec((tf, d), lambda i, j: (j, 0)),
                  pl.BlockSpec((1, d), lambda i, j: (0, 0)),
                  pl.BlockSpec((1, d), lambda i, j: (0, 0))],
        out_specs=pl.BlockSpec((tm, d), lambda i, j: (i, 0)),
        out_shape=jax.ShapeDtypeStruct((t, d), F32),
        scratch_shapes=[pltpu.VMEM((tm, d), BF16)],
        compiler_params=_params("parallel", "arbitrary"),
        name="mlp",
    )(x, w1, w2, g, b)


def _qk_rope_kernel(x_ref, w_ref, gain_ref, cos_ref, sin_ref, o_ref, xb_ref):
    @pl.when(pl.program_id(1) == 0)
    def _():
        xb_ref[...] = x_ref[...].astype(BF16)

    pair = 2 * HEAD_DIM
    cos = jnp.tile(cos_ref[...], (1, 2))
    sin = jnp.tile(sin_ref[...], (1, 2))
    src = lax.broadcasted_iota(jnp.int32, (pair, pair), 0)
    dst = lax.broadcasted_iota(jnp.int32, (pair, pair), 1)
    quarter = HEAD_DIM // 4
    partner = jnp.where((dst % (2 * quarter)) < quarter, dst + quarter, dst - quarter)
    perm = jnp.where(src == partner, 1.0, 0.0).astype(BF16)
    y = _dot(xb_ref[...], w_ref[...])
    for c in range(o_ref.shape[1] // pair):
        heads = []
        for h in range(2):
            cols = slice(c * pair + h * HEAD_DIM, c * pair + (h + 1) * HEAD_DIM)
            t = y[:, cols]
            heads.append(t * lax.rsqrt(jnp.mean(t * t, axis=-1, keepdims=True) + EPS) * gain_ref[:, cols])
        t = jnp.concatenate(heads, axis=-1)
        t_hi = t.astype(BF16)
        t_lo = (t - t_hi.astype(F32)).astype(BF16)
        rot = _dot(t_hi, perm) + _dot(t_lo, perm)
        o_ref[:, c * pair:(c + 1) * pair] = (t * cos + rot * sin).astype(o_ref.dtype)


def _qk_rope(x, w, gains, cos, sin, seq, tm=512, tn=512):
    t, k = x.shape
    n = w.shape[1]
    tm, tn = _tile(seq, tm), _tile(n, tn)
    ns = seq // tm
    return pl.pallas_call(
        _qk_rope_kernel,
        grid=(t // tm, n // tn),
        in_specs=[pl.BlockSpec((tm, k), lambda i, j: (i, 0)),
                  pl.BlockSpec((k, tn), lambda i, j: (0, j)),
                  pl.BlockSpec((1, tn), lambda i, j: (0, j)),
                  pl.BlockSpec((tm, HEAD_DIM), lambda i, j: (i % ns, 0)),
                  pl.BlockSpec((tm, HEAD_DIM), lambda i, j: (i % ns, 0))],
        out_specs=pl.BlockSpec((tm, tn), lambda i, j: (i, j)),
        out_shape=jax.ShapeDtypeStruct((t, n), BF16),
        scratch_shapes=[pltpu.VMEM((tm, k), BF16)],
        compiler_params=_params("parallel", "arbitrary"),
        name="qk_rope",
    )(x, w, gains, cos, sin)


SAFE_LOG2 = 60.0


def _gqa_kernel(safe_ref, q_ref, k_ref, v_ref, o_ref, vx_ref, m_ref, acc_ref, *, kb):
    nkb = k_ref.shape[0] // kb

    @pl.when(pl.program_id(2) == 0)
    def _():
        vx_ref[:, :HEAD_DIM] = v_ref[...]
        vx_ref[:, HEAD_DIM:] = jnp.ones_like(v_ref)

    safe = safe_ref[0] != 0

    @pl.when(safe)
    def _():
        def block(j, first):
            ks = pl.ds(pl.multiple_of(j * kb, kb), kb)
            k_blk = k_ref[ks, :]
            vx_blk = vx_ref[ks, :]
            for g in range(GQA_GROUP):
                s = _dot_nt(q_ref[:, g * HEAD_DIM:(g + 1) * HEAD_DIM], k_blk)
                pv = _dot(jnp.exp2(s).astype(BF16), vx_blk)
                acc_ref[g] = pv if first else acc_ref[g] + pv

        block(0, True)

        def body(j, carry):
            block(j, False)
            return carry

        lax.fori_loop(1, nkb, body, 0)

    @pl.when(jnp.logical_not(safe))
    def _():
        m_ref[...] = jnp.full_like(m_ref, -jnp.inf)
        acc_ref[...] = jnp.zeros_like(acc_ref)

        def body(j, carry):
            ks = pl.ds(pl.multiple_of(j * kb, kb), kb)
            k_blk = k_ref[ks, :]
            vx_blk = vx_ref[ks, :]
            for g in range(GQA_GROUP):
                s = _dot_nt(q_ref[:, g * HEAD_DIM:(g + 1) * HEAD_DIM], k_blk)
                m_prev = m_ref[g]
                m_new = jnp.maximum(m_prev, jnp.max(s, axis=-1, keepdims=True))
                alpha = jnp.exp2(m_prev - m_new)
                p = jnp.exp2(s - jnp.tile(m_new, (1, kb // LANES)))
                acc_ref[g] = jnp.tile(alpha, (1, 2)) * acc_ref[g] + _dot(p.astype(BF16), vx_blk)
                m_ref[g] = m_new
            return carry

        lax.fori_loop(0, nkb, body, 0)

    for g in range(GQA_GROUP):
        acc = acc_ref[g]
        o_ref[:, g * HEAD_DIM:(g + 1) * HEAD_DIM] = (acc[:, :HEAD_DIM] / acc[:, HEAD_DIM:]).astype(o_ref.dtype)


def _gqa(safe, qk, v, batch, seq, d_model, tq=512, kb=2048):
    n_kv = d_model // HEAD_DIM // GQA_GROUP
    tq, kb = _tile(seq, tq), _tile(seq, kb)
    nq = seq // tq
    gw = GQA_GROUP * HEAD_DIM
    k0 = d_model // HEAD_DIM
    scratch = [pltpu.VMEM((seq, 2 * HEAD_DIM), BF16),
               pltpu.VMEM((GQA_GROUP, tq, HEAD_DIM), F32),
               pltpu.VMEM((GQA_GROUP, tq, 2 * HEAD_DIM), F32)]
    return pl.pallas_call(
        functools.partial(_gqa_kernel, kb=kb),
        grid=(batch, n_kv, nq),
        in_specs=[pl.BlockSpec(memory_space=pltpu.SMEM),
                  pl.BlockSpec((tq, gw), lambda b, h, i: (b * nq + i, h)),
                  pl.BlockSpec((seq, HEAD_DIM), lambda b, h, i: (b, k0 + h)),
                  pl.BlockSpec((seq, HEAD_DIM), lambda b, h, i: (b, h))],
        out_specs=pl.BlockSpec((tq, gw), lambda b, h, i: (b * nq + i, h)),
        out_shape=jax.ShapeDtypeStruct((batch * seq, d_model), BF16),
        scratch_shapes=scratch,
        compiler_params=_params("parallel", "parallel", "arbitrary"),
        name="gqa",
    )(safe, qk, qk, v)


def _gqa_safe_flag(q_gain_scaled, k_gain):
    bound = 1.02 * HEAD_DIM * jnp.max(jnp.abs(q_gain_scaled), axis=-1) * jnp.max(jnp.abs(k_gain), axis=-1)
    return (bound <= SAFE_LOG2).astype(jnp.int32)[:, None]


def _rope_tables(n):
    axis_dim = HEAD_DIM // 2
    rows = n // GRID_W
    row = jnp.repeat(jnp.arange(rows, dtype=F32), GRID_W)
    col = jnp.tile(jnp.arange(GRID_W, dtype=F32), rows)
    inv = ROPE_THETA ** (-jnp.arange(0, axis_dim, 2, dtype=F32) / axis_dim)
    ar = row[:, None] * inv
    ac = col[:, None] * inv
    ang = jnp.concatenate([ar, ar, ac, ac], axis=-1)
    lane = jnp.arange(HEAD_DIM)
    sign = jnp.where((lane % axis_dim) < axis_dim // 2, -1.0, 1.0).astype(F32)
    return jnp.cos(ang), jnp.sin(ang) * sign


def _prep_weights(gla_w_in, gla_w_gate_up, gla_b_gate, gla_norm_g, gla_w_out,
                  att_w_qkv, att_q_gain, att_k_gain, att_w_out,
                  mem_w_q, mem_w_kv, mem_w_o, mlp_w1, mlp_w2, ln_g, ln_b):
    d_model = gla_w_in.shape[1]
    n_main = 3 * d_model
    dk_all = d_model // 2
    p = {}
    p["gla_w_main"] = gla_w_in[:, :, :n_main].astype(BF16)
    wz = gla_w_in[:, :, n_main:]
    p["gla_w_z"] = jnp.pad(wz, ((0, 0), (0, 0), (0, LANES - wz.shape[-1]))).astype(BF16)
    n_gla = gla_w_gate_up.shape[0]
    wup = jnp.zeros((n_gla, 2, LANES, dk_all), F32)
    wup = wup.at[:, 0, :GATE_RANK].set(gla_w_gate_up[:, 0])
    wup = wup.at[:, 1, GATE_RANK:2 * GATE_RANK].set(gla_w_gate_up[:, 1])
    p["gla_wup"] = wup.astype(BF16)
    p["gla_bg"] = gla_b_gate.reshape(n_gla, 2, 1, dk_all)
    p["gla_ng"] = gla_norm_g.reshape(n_gla, 1, -1)
    p["gla_w_out"] = gla_w_out.astype(BF16)
    n_q = d_model // HEAD_DIM
    n_kv = n_q // GQA_GROUP
    n_qk = (n_q + n_kv) * HEAD_DIM
    p["att_w_qk"] = att_w_qkv[:, :, :n_qk].astype(BF16)
    p["att_w_v"] = att_w_qkv[:, :, n_qk:].astype(BF16)
    q_gain = att_q_gain * (HEAD_DIM ** -0.5 * math.log2(math.e))
    p["att_safe"] = _gqa_safe_flag(q_gain, att_k_gain)
    gains = jnp.concatenate([jnp.tile(q_gain, (1, n_q)), jnp.tile(att_k_gain, (1, n_kv))], axis=-1)
    p["att_gains"] = gains[:, None, :]
    p["att_w_out"] = att_w_out.astype(BF16)
    p["mem_w_q"] = mem_w_q.astype(BF16)
    p["mem_w_kv"] = mem_w_kv.astype(BF16)
    p["mem_w_o"] = mem_w_o.astype(BF16)
    p["mlp_w1"] = mlp_w1.astype(BF16)
    p["mlp_w2"] = mlp_w2.astype(BF16)
    p["ln_g"] = ln_g[:, :, None, :]
    p["ln_b"] = ln_b[:, :, None, :]
    return p


def _trunk(x3, mem3, p):
    batch, seq, d_model = x3.shape
    n_mem = mem3.shape[1]
    x = x3.reshape(batch * seq, d_model)
    mem = mem3.reshape(batch * n_mem, d_model)
    cos, sin = _rope_tables(seq)
    for i in range(DEPTH):
        j = i // 2
        if i % 2 == 0:
            proj, z = _gla_proj(x, p["gla_w_main"][j], p["gla_w_z"][j])
            o_f, o_b = _gla_scan(proj, z, p["gla_wup"][j], p["gla_bg"][j], batch, seq, d_model)
            x = _gla_out(o_f, o_b, proj, p["gla_ng"][j], p["gla_w_out"][j], x,
                         p["ln_g"][i, 0], p["ln_b"][i, 0])
        else:
            qk = _qk_rope(x, p["att_w_qk"][j], p["att_gains"][j], cos, sin, seq)
            v = _matmul(x, p["att_w_v"][j], BF16)
            h = _gqa(p["att_safe"][j], qk, v, batch, seq, d_model)
            x = _proj_ln(h, p["att_w_out"][j], x, p["ln_g"][i, 0], p["ln_b"][i, 0])
        kv = _matmul(mem, p["mem_w_kv"][i], BF16)
        x = _xattn(x, kv, p["mem_w_q"][i], p["mem_w_o"][i], p["ln_g"][i, 1], p["ln_b"][i, 1],
                   batch, seq, n_mem)
        x = _mlp(x, p["mlp_w1"][i], p["mlp_w2"][i], p["ln_g"][i, 2], p["ln_b"][i, 2])
    return x.reshape(batch, seq, d_model)


def kernel(x_prompt, x_sample, mem_prompt, mem_sample, gla_w_in, gla_w_gate_up, gla_b_gate, gla_norm_g, gla_w_out, att_w_qkv, att_q_gain, att_k_gain, att_w_out, mem_w_q, mem_w_kv, mem_w_o, mlp_w1, mlp_w2, ln_g, ln_b):
    p = _prep_weights(gla_w_in, gla_w_gate_up, gla_b_gate, gla_norm_g, gla_w_out,
                      att_w_qkv, att_q_gain, att_k_gain, att_w_out,
                      mem_w_q, mem_w_kv, mem_w_o, mlp_w1, mlp_w2, ln_g, ln_b)
    return (_trunk(x_prompt, mem_prompt, p), _trunk(x_sample, mem_sample, p))
```
